```python
import math
import jax, jax.numpy as jnp
from jax import lax
import numpy as np

D_MODEL = 1024
BATCH = 16
SEQ = 2048
DEPTH = 4

D_FF = 2816
NORM_EPS = 1e-6
F_MIN = 1e-6
CHUNK = 64
HG_HEADS = 4
HG_DK = 128
HG_DV = 128
HG_QK = HG_HEADS * HG_DK
HG_WIDTH = HG_HEADS * HG_DV
S5_WIDTH = D_MODEL - HG_WIDTH
S5_GROUP = 16
S5_GROUPS = S5_WIDTH // S5_GROUP
S5_STATE = 64
EV_IN = 2 * HG_QK + 2 * HG_WIDTH + S5_WIDTH
GDN_HEADS = 8
GDN_DK = 128
GDN_DV = 128
GDN_QK = GDN_HEADS * GDN_DK
GDN_V = GDN_HEADS * GDN_DV
CONV_W = 4
OD_IN = 2 * GDN_QK + 2 * GDN_V + 2 * GDN_HEADS
N_EVEN = (DEPTH + 1) // 2
N_ODD = DEPTH // 2

kernel_name = 'hybrid_hgrn2_s5_gdn_macaron'


def rmsnorm(x, w):
    xf = x.astype(jnp.float32)
    y = xf * lax.rsqrt(jnp.mean(xf * xf, axis=-1, keepdims=True) + NORM_EPS)
    return (y * w.astype(jnp.float32)).astype(x.dtype)


def swiglu(h, w_gate, w_up, w_down):
    return (jax.nn.silu(h @ w_gate) * (h @ w_up)) @ w_down


def split_chunks(t, n_heads):
    b, l, _ = t.shape
    t = t.reshape(b, l // CHUNK, CHUNK, n_heads, -1)
    return t.transpose(1, 0, 3, 2, 4)


def merge_chunks(t):
    nc, b, h, c, d = t.shape
    return t.transpose(1, 0, 3, 2, 4).reshape(b, nc * c, h, d)


def masked_exp(mask, z):
    return jnp.where(mask, jnp.exp(jnp.where(mask, z, 0.0)), 0.0)


def gated_head_norm(o, gate, w):
    o = o * lax.rsqrt(jnp.mean(o * o, axis=-1, keepdims=True) + NORM_EPS) * w.astype(jnp.float32)
    o = o * jax.nn.silu(gate.astype(jnp.float32)).reshape(o.shape)
    return o.reshape(o.shape[0], o.shape[1], -1)


def l2norm(t):
    return t * lax.rsqrt(jnp.sum(t * t, axis=-1, keepdims=True) + NORM_EPS)


def hgrn2_mix(q_lin, f_lin, i_val, g_lin, lb, norm_w):
    f32 = jnp.float32
    q = jax.nn.silu(q_lin.astype(f32))
    f = lb + (1.0 - lb) * jax.nn.sigmoid(f_lin.astype(f32))
    log_f = jnp.log(jnp.maximum(f, F_MIN))
    k = 1.0 - f
    v = i_val.astype(f32)
    qc = split_chunks(q, HG_HEADS)
    kc = split_chunks(k, HG_HEADS)
    vc = split_chunks(v, HG_HEADS)
    lfc = split_chunks(log_f, HG_HEADS)
    causal = jnp.tril(jnp.ones((CHUNK, CHUNK), dtype=bool))[:, :, None]
    s0 = jnp.zeros((q.shape[0], HG_HEADS, HG_DK, HG_DV), f32)

    def step(S, inp):
        q_c, k_c, v_c, lf_c = inp
        b = jnp.cumsum(lf_c, axis=2)
        diff = b[:, :, :, None, :] - b[:, :, None, :, :]
        decay = masked_exp(causal, diff)
        attn = jnp.einsum('bhtk,bhtsk,bhsk->bhts', q_c, decay, k_c)
        o = attn @ v_c + jnp.einsum('bhtk,bhkv->bhtv', q_c * jnp.exp(b), S)
        b_last = b[:, :, -1:, :]
        S = jnp.exp(b_last[:, :, 0, :, None]) * S + jnp.einsum(
            'bhsk,bhsv->bhkv', k_c * jnp.exp(b_last - b), v_c)
        return S, o

    _, o = lax.scan(step, s0, (qc, kc, vc, lfc))
    return gated_head_norm(merge_chunks(o), g_lin, norm_w)


def s5_mix(u, a_re, a_im, b_re, b_im, c_re, c_im, d, log_dt, w_glu):
    f32 = jnp.float32
    bsz, l, _ = u.shape
    uf = u.astype(f32).reshape(bsz, l, S5_GROUPS, S5_GROUP)
    a_re = a_re.astype(f32)
    a_im = a_im.astype(f32)
    dt = jnp.exp(log_dt.astype(f32))[:, None]
    mag = jnp.exp(dt * a_re)
    ang = dt * a_im
    abar_re = mag * jnp.cos(ang)
    abar_im = mag * jnp.sin(ang)
    den = a_re * a_re + a_im * a_im
    zr = abar_re - 1.0
    zi = abar_im
    coef_re = ((zr * a_re + zi * a_im) / den)[..., None]
    coef_im = ((zi * a_re - zr * a_im) / den)[..., None]
    b_re = b_re.astype(f32)
    b_im = b_im.astype(f32)
    bb_re = coef_re * b_re - coef_im * b_im
    bb_im = coef_re * b_im + coef_im * b_re
    bu_re = jnp.einsum('blgp,gnp->blgn', uf, bb_re)
    bu_im = jnp.einsum('blgp,gnp->blgn', uf, bb_im)
    ar = jnp.broadcast_to(abar_re, (1, l, S5_GROUPS, S5_STATE))
    ai = jnp.broadcast_to(abar_im, (1, l, S5_GROUPS, S5_STATE))

    def combine(e1, e2):
        ar1, ai1, br1, bi1 = e1
        ar2, ai2, br2, bi2 = e2
        return (ar2 * ar1 - ai2 * ai1,
                ar2 * ai1 + ai2 * ar1,
                ar2 * br1 - ai2 * bi1 + br2,
                ar2 * bi1 + ai2 * br1 + bi2)

    _, _, h_re, h_im = lax.associative_scan(combine, (ar, ai, bu_re, bu_im), axis=1)
    y = (jnp.einsum('gpn,blgn->blgp', c_re.astype(f32), h_re)
         - jnp.einsum('gpn,blgn->blgp', c_im.astype(f32), h_im)
         + d.astype(f32).reshape(S5_GROUPS, S5_GROUP) * uf)
    y = jax.nn.gelu(y.reshape(bsz, l, S5_WIDTH))
    return y * jax.nn.sigmoid(y @ w_glu.astype(f32))


def causal_dwconv(x, w):
    return lax.conv_general_dilated(
        x, w[:, None, :].astype(x.dtype), window_strides=(1,), padding=[(CONV_W - 1, 0)],
        dimension_numbers=('NWC', 'WIO', 'NWC'), feature_group_count=x.shape[-1])


def gdn_mix(proj, conv_w, a_log, dt_bias, norm_w):
    f32 = jnp.float32
    n_qkv = 2 * GDN_QK + GDN_V
    qkv = jax.nn.silu(causal_dwconv(proj[..., :n_qkv], conv_w)).astype(f32)
    gate = proj[..., n_qkv:n_qkv + GDN_V]
    beta = jax.nn.sigmoid(proj[..., n_qkv + GDN_V:n_qkv + GDN_V + GDN_HEADS].astype(f32))
    a_lin = proj[..., n_qkv + GDN_V + GDN_HEADS:].astype(f32)
    log_alpha = -jnp.exp(a_log.astype(f32)) * jax.nn.softplus(a_lin + dt_bias.astype(f32))
    q = split_chunks(qkv[..., :GDN_QK], GDN_HEADS)
    k = split_chunks(qkv[..., GDN_QK:2 * GDN_QK], GDN_HEADS)
    v = split_chunks(qkv[..., 2 * GDN_QK:], GDN_HEADS)
    q = l2norm(q) * (GDN_DK ** -0.5)
    k = l2norm(k)
    beta = split_chunks(beta, GDN_HEADS)[..., 0]
    g = jnp.cumsum(split_chunks(log_alpha, GDN_HEADS)[..., 0], axis=-1)
    lower = jnp.tril(jnp.ones((CHUNK, CHUNK), dtype=bool))
    strict = jnp.tril(jnp.ones((CHUNK, CHUNK), dtype=bool), k=-1)
    l_mask = masked_exp(lower, g[..., :, None] - g[..., None, :])
    kb = k * beta[..., None]
    vb = v * beta[..., None]
    m = jnp.where(strict, jnp.einsum('...ik,...jk->...ij', kb, k) * l_mask, 0.0)
    eye = jnp.eye(CHUNK, dtype=f32)
    t_inv = lax.linalg.triangular_solve(eye + m, jnp.broadcast_to(eye, m.shape),
                                        left_side=True, lower=True, unit_diagonal=True)
    u_c = t_inv @ vb
    w_c = t_inv @ (kb * jnp.exp(g)[..., None])
    attn = jnp.einsum('...ik,...jk->...ij', q, k) * l_mask
    s0 = jnp.zeros((proj.shape[0], GDN_HEADS, GDN_DK, GDN_DV), f32)

    def step(S, inp):
        q_c, k_c, uu, ww, at, g_c = inp
        v_new = uu - ww @ S
        o = (q_c * jnp.exp(g_c)[..., None]) @ S + at @ v_new
        g_last = g_c[..., -1:]
        S = S * jnp.exp(g_last)[..., None] + jnp.einsum(
            'bhsk,bhsv->bhkv', k_c * jnp.exp(g_last - g_c)[..., None], v_new)
        return S, o

    _, o = lax.scan(step, s0, (q, k, u_c, w_c, attn, g))
    return gated_head_norm(merge_chunks(o), gate, norm_w)


def setup_inputs(seed: int = 0) -> dict:
    key = jax.random.key(seed)
    ks = jax.random.split(key, 32)
    f32 = jnp.float32

    def nrm(k, shape, scale):
        return jax.random.normal(k, shape, f32) * scale

    def gain(k, shape):
        return 1.0 + 0.02 * jax.random.normal(k, shape, f32)

    n_idx = jnp.arange(S5_STATE, dtype=f32)
    gdn_dt = jnp.exp(jax.random.uniform(ks[27], (N_ODD, GDN_HEADS), f32, math.log(1e-3), math.log(1e-1)))
    return {
        'x': nrm(ks[0], (BATCH, SEQ, D_MODEL), 1.0),
        'ffn1_norm': gain(ks[1], (DEPTH, D_MODEL)),
        'ffn1_w_gate': nrm(ks[2], (DEPTH, D_MODEL, D_FF), D_MODEL ** -0.5),
        'ffn1_w_up': nrm(ks[3], (DEPTH, D_MODEL, D_FF), D_MODEL ** -0.5),
        'ffn1_w_down': nrm(ks[4], (DEPTH, D_FF, D_MODEL), D_FF ** -0.5),
        'mix_norm': gain(ks[5], (DEPTH, D_MODEL)),
        'ffn2_norm': gain(ks[6], (DEPTH, D_MODEL)),
        'ffn2_w_gate': nrm(ks[7], (DEPTH, D_MODEL, D_FF), D_MODEL ** -0.5),
        'ffn2_w_up': nrm(ks[8], (DEPTH, D_MODEL, D_FF), D_MODEL ** -0.5),
        'ffn2_w_down': nrm(ks[9], (DEPTH, D_FF, D_MODEL), D_FF ** -0.5),
        'ev_w_in': nrm(ks[10], (N_EVEN, D_MODEL, EV_IN), D_MODEL ** -0.5),
        'hg_lb_logits': nrm(ks[11], (N_EVEN, HG_QK), 0.1),
        'hg_norm_w': gain(ks[12], (N_EVEN, HG_DV)),
        's5_a_re': -0.5 + nrm(ks[13], (N_EVEN, S5_GROUPS, S5_STATE), 0.01),
        's5_a_im': math.pi * n_idx + nrm(ks[14], (N_EVEN, S5_GROUPS, S5_STATE), 0.01),
        's5_b_re': nrm(ks[15], (N_EVEN, S5_GROUPS, S5_STATE, S5_GROUP), (2 * S5_GROUP) ** -0.5),
        's5_b_im': nrm(ks[16], (N_EVEN, S5_GROUPS, S5_STATE, S5_GROUP), (2 * S5_GROUP) ** -0.5),
        's5_c_re': nrm(ks[17], (N_EVEN, S5_GROUPS, S5_GROUP, S5_STATE), (2 * S5_STATE) ** -0.5),
        's5_c_im': nrm(ks[18], (N_EVEN, S5_GROUPS, S5_GROUP, S5_STATE), (2 * S5_STATE) ** -0.5),
        's5_d': nrm(ks[19], (N_EVEN, S5_WIDTH), 1.0),
        's5_log_dt': jax.random.uniform(ks[20], (N_EVEN, S5_GROUPS), f32, math.log(1e-3), math.log(1e-1)),
        's5_w_glu': nrm(ks[21], (N_EVEN, S5_WIDTH, S5_WIDTH), S5_WIDTH ** -0.5),
        'ev_w_out': nrm(ks[22], (N_EVEN, HG_WIDTH + S5_WIDTH, D_MODEL), (HG_WIDTH + S5_WIDTH) ** -0.5),
        'od_w_in': nrm(ks[23], (N_ODD, D_MODEL, OD_IN), D_MODEL ** -0.5),
        'gdn_conv_w': nrm(ks[24], (N_ODD, CONV_W, 2 * GDN_QK + GDN_V), CONV_W ** -0.5),
        'gdn_a_log': jnp.log(jax.random.uniform(ks[25], (N_ODD, GDN_HEADS), f32, 1.0, 16.0)),
        'gdn_dt_bias': gdn_dt + jnp.log(-jnp.expm1(-gdn_dt)),
        'gdn_norm_w': gain(ks[26], (N_ODD, GDN_DV)),
        'od_w_out': nrm(ks[28], (N_ODD, GDN_V, D_MODEL), GDN_V ** -0.5),
        'final_norm': gain(ks[29], (D_MODEL,)),
    }


def reference(x, ffn1_norm, ffn1_w_gate, ffn1_w_up, ffn1_w_down, mix_norm,
              ffn2_norm, ffn2_w_gate, ffn2_w_up, ffn2_w_down,
              ev_w_in, hg_lb_logits, hg_norm_w, s5_a_re, s5_a_im, s5_b_re, s5_b_im,
              s5_c_re, s5_c_im, s5_d, s5_log_dt, s5_w_glu, ev_w_out,
              od_w_in, gdn_conv_w, gdn_a_log, gdn_dt_bias, gdn_norm_w, od_w_out,
              final_norm):
    p = jax.nn.softmax(hg_lb_logits.astype(jnp.float32), axis=0)
    lbs = jnp.cumsum(p, axis=0) - p[0]
    for layer in range(DEPTH):
        x = x + 0.5 * swiglu(rmsnorm(x, ffn1_norm[layer]), ffn1_w_gate[layer],
                             ffn1_w_up[layer], ffn1_w_down[layer])
        h = rmsnorm(x, mix_norm[layer])
        j = layer // 2
        if layer % 2 == 0:
            proj = h @ ev_w_in[j]
            y_a = hgrn2_mix(proj[..., :HG_QK],
                            proj[..., HG_QK:2 * HG_QK],
                            proj[..., 2 * HG_QK:2 * HG_QK + HG_WIDTH],
                            proj[..., 2 * HG_QK + HG_WIDTH:2 * HG_QK + 2 * HG_WIDTH],
                            lbs[j], hg_norm_w[j])
            y_b = s5_mix(proj[..., 2 * HG_QK + 2 * HG_WIDTH:], s5_a_re[j], s5_a_im[j],
                         s5_b_re[j], s5_b_im[j], s5_c_re[j], s5_c_im[j], s5_d[j],
                         s5_log_dt[j], s5_w_glu[j])
            y = (jnp.concatenate([y_a, y_b], axis=-1) @ ev_w_out[j]).astype(x.dtype)
        else:
            proj = h @ od_w_in[j]
            y = (gdn_mix(proj, gdn_conv_w[j], gdn_a_log[j], gdn_dt_bias[j], gdn_norm_w[j])
                 @ od_w_out[j]).astype(x.dtype)
        x = x + y
        x = x + 0.5 * swiglu(rmsnorm(x, ffn2_norm[layer]), ffn2_w_gate[layer],
                             ffn2_w_up[layer], ffn2_w_down[layer])
    return rmsnorm(x, final_norm)
```

```python
import functools

import jax
import jax.numpy as jnp
from jax import lax
from jax.experimental import pallas as pl
from jax.experimental.pallas import tpu as pltpu

F32 = jnp.float32
BF16 = jnp.bfloat16
HIGHEST = lax.Precision.HIGHEST

D_MODEL = 1024
DEPTH = 4
D_FF = 2816
NORM_EPS = 1e-6
F_MIN = 1e-6
CHUNK = 64
HG_HEADS = 4
HG_DK = 128
HG_DV = 128
HG_QK = HG_HEADS * HG_DK
HG_WIDTH = HG_HEADS * HG_DV
S5_WIDTH = D_MODEL - HG_WIDTH
S5_GROUP = 16
S5_GROUPS = S5_WIDTH // S5_GROUP
S5_STATE = 64
S5_NSTATE = S5_GROUPS * S5_STATE
GDN_HEADS = 8
GDN_DK = 128
GDN_DV = 128
GDN_QK = GDN_HEADS * GDN_DK
GDN_V = GDN_HEADS * GDN_DV
CONV_W = 4

LANES = 128
SUBLANES = 8
VMEM_LIMIT_BYTES = 56 * 1024 * 1024

TOKEN_TILE = 512
FF_BLOCK = 256
MIX_TILE = 256
S5_TILE = 64
S5_LANE_CHUNK = 128
S5_STATE_CHUNK = S5_LANE_CHUNK // S5_GROUP * S5_STATE


def _mm(a, b):
    return jnp.dot(a.astype(BF16), b.astype(BF16), preferred_element_type=F32)


def _mm_nt(a, b):
    return lax.dot_general(a.astype(BF16), b.astype(BF16), (((1,), (1,)), ((), ())),
                           preferred_element_type=F32)


def _mm_tn(a, b):
    return lax.dot_general(a.astype(BF16), b.astype(BF16), (((0,), (0,)), ((), ())),
                           preferred_element_type=F32)


def _mm_exact(a, b):
    return jnp.dot(a, b, precision=HIGHEST, preferred_element_type=F32)


def _sigmoid(x):
    return 1.0 / (1.0 + jnp.exp(-x))


def _silu(x):
    return x * _sigmoid(x)


def _rmsnorm(x, w):
    return x * lax.rsqrt(jnp.mean(x * x, axis=-1, keepdims=True) + NORM_EPS) * w


def _resident(shape):
    nd = len(shape)
    return pl.BlockSpec(shape, lambda *_: (0,) * nd, pipeline_mode=pl.Buffered(1))


def _params(*sem):
    return pltpu.CompilerParams(dimension_semantics=sem, vmem_limit_bytes=VMEM_LIMIT_BYTES)


def _ffn_kernel(*refs, final):
    if final:
        x_ref, nw_ref, wg_ref, wu_ref, wd_ref, fw_ref, o_ref = refs
    else:
        x_ref, nw_ref, wg_ref, wu_ref, wd_ref, o_ref = refs
    x = x_ref[...]
    h = _rmsnorm(x, nw_ref[...]).astype(BF16)
    acc = jnp.zeros(x.shape, F32)
    for j in range(D_FF // FF_BLOCK):
        sl = slice(j * FF_BLOCK, (j + 1) * FF_BLOCK)
        g = jnp.dot(h, wg_ref[:, sl], preferred_element_type=F32)
        u = jnp.dot(h, wu_ref[:, sl], preferred_element_type=F32)
        a = (_silu(g) * u).astype(BF16)
        acc = acc + jnp.dot(a, wd_ref[sl, :], preferred_element_type=F32)
    y = x + 0.5 * acc
    if final:
        y = _rmsnorm(y, fw_ref[...])
    o_ref[...] = y


def ffn(x, norm_w, w_gate, w_up, w_down, final_w=None):
    t = x.shape[0]
    tm = min(TOKEN_TILE, t)
    final = final_w is not None
    row = pl.BlockSpec((tm, D_MODEL), lambda i: (i, 0))
    in_specs = [row, _resident((1, D_MODEL)), _resident((D_MODEL, D_FF)),
                _resident((D_MODEL, D_FF)), _resident((D_FF, D_MODEL))]
    args = [x, norm_w.reshape(1, D_MODEL), w_gate.astype(BF16), w_up.astype(BF16),
            w_down.astype(BF16)]
    if final:
        in_specs.append(_resident((1, D_MODEL)))
        args.append(final_w.reshape(1, D_MODEL))
    return pl.pallas_call(
        functools.partial(_ffn_kernel, final=final),
        grid=(t // tm,),
        in_specs=in_specs,
        out_specs=row,
        out_shape=jax.ShapeDtypeStruct((t, D_MODEL), F32),
        compiler_params=_params("parallel"),
        name="ffn_final" if final else "ffn",
    )(*args)


def _norm_proj_kernel(x_ref, nw_ref, w_ref, *o_refs, splits):
    h = _rmsnorm(x_ref[...], nw_ref[...]).astype(BF16)
    off = 0
    for o_ref, n in zip(o_refs, splits):
        o_ref[...] = jnp.dot(h, w_ref[:, off:off + n], preferred_element_type=F32)
        off += n


def norm_proj(x, norm_w, w, splits, seq, time_major_last=False):
    t = x.shape[0]
    tm = min(TOKEN_TILE, seq)
    n_total = sum(splits)
    blocks_per_seq = seq // tm
    out_specs, out_shapes = [], []
    for idx, n in enumerate(splits):
        if time_major_last and idx == len(splits) - 1:
            out_specs.append(pl.BlockSpec(
                (tm, n), lambda i: (i % blocks_per_seq, i // blocks_per_seq)))
            out_shapes.append(jax.ShapeDtypeStruct((seq, (t // seq) * n), F32))
        else:
            out_specs.append(pl.BlockSpec((tm, n), lambda i: (i, 0)))
            out_shapes.append(jax.ShapeDtypeStruct((t, n), F32))
    return pl.pallas_call(
        functools.partial(_norm_proj_kernel, splits=splits),
        grid=(t // tm,),
        in_specs=[pl.BlockSpec((tm, D_MODEL), lambda i: (i, 0)), _resident((1, D_MODEL)),
                  _resident((D_MODEL, n_total))],
        out_specs=out_specs,
        out_shape=out_shapes,
        compiler_params=_params("parallel"),
        name="norm_proj",
    )(x, norm_w.reshape(1, D_MODEL), w.astype(BF16))


def _out_proj_kernel(x_ref, *refs):
    o_ref = refs[-1]
    acc = x_ref[...]
    for y_ref, w_ref in zip(refs[0:-1:2], refs[1:-1:2]):
        acc = acc + jnp.dot(y_ref[...].astype(BF16), w_ref[...], preferred_element_type=F32)
    o_ref[...] = acc


def out_proj(x, parts, seq):
    t = x.shape[0]
    tm = min(TOKEN_TILE, seq)
    blocks_per_seq = seq // tm
    row = pl.BlockSpec((tm, D_MODEL), lambda i: (i, 0))
    in_specs, args = [row], [x]
    for y, w, time_major in parts:
        n = w.shape[0]
        if time_major:
            in_specs.append(pl.BlockSpec(
                (tm, n), lambda i: (i % blocks_per_seq, i // blocks_per_seq)))
        else:
            in_specs.append(pl.BlockSpec((tm, n), lambda i: (i, 0)))
        in_specs.append(_resident((n, D_MODEL)))
        args += [y, w.astype(BF16)]
    return pl.pallas_call(
        _out_proj_kernel,
        grid=(t // tm,),
        in_specs=in_specs,
        out_specs=row,
        out_shape=jax.ShapeDtypeStruct((t, D_MODEL), F32),
        compiler_params=_params("parallel"),
        name="out_proj",
    )(*args)


def _iota2(shape, axis):
    return lax.broadcasted_iota(jnp.int32, shape, axis)


def _hgrn2_kernel(q_ref, f_ref, i_ref, g_ref, lbl_ref, nw_ref, o_ref, st_ref, *, layer, n_chunks):
    @pl.when(pl.program_id(1) == 0)
    def _():
        st_ref[...] = jnp.zeros(st_ref.shape, F32)

    logits = lbl_ref[...]
    e = jnp.exp(logits - jnp.max(logits, axis=0, keepdims=True))
    p = e / jnp.sum(e, axis=0, keepdims=True)
    lb = jnp.sum(p[0:layer + 1], axis=0, keepdims=True) - p[0:1]
    nw = nw_ref[...]

    rt = _iota2((CHUNK, CHUNK), 0)
    ct = _iota2((CHUNK, CHUNK), 1)
    tri = (rt >= ct).astype(F32)
    levels = (32, 16, 8)
    level_mask = {m: ((rt // m) % 2 == 1) & (ct // m == rt // m - 1) for m in levels}
    nblk = CHUNK // SUBLANES
    t_local = _iota2((nblk, SUBLANES, 1), 1)

    def chunk_body(c, carry):
        rows = pl.ds(pl.multiple_of(c * CHUNK, CHUNK), CHUNK)
        q = _silu(q_ref[rows, :])
        f = lb + (1.0 - lb) * _sigmoid(f_ref[rows, :])
        lf = jnp.log(jnp.maximum(f, F_MIN))
        kk = 1.0 - f
        v = i_ref[rows, :]
        gate = g_ref[rows, :]
        b = _mm_exact(tri, lf)
        outs = []
        for h in range(HG_HEADS):
            sl = slice(h * HG_DK, (h + 1) * HG_DK)
            bh, qh, kh, vh = b[:, sl], q[:, sl], kk[:, sl], v[:, sl]
            attn = jnp.zeros((CHUNK, CHUNK), F32)
            for m in levels:
                nb = CHUNK // m
                b3 = bh.reshape(nb, m, HG_DK)
                bend = b3[:, m - 1:m, :]
                bprev = jnp.concatenate([jnp.zeros((1, 1, HG_DK), F32), bend[:-1]], axis=0)
                qt = qh * jnp.exp(b3 - bprev).reshape(CHUNK, HG_DK)
                kt = kh * jnp.exp(bend - b3).reshape(CHUNK, HG_DK)
                attn = attn + jnp.where(level_mask[m], _mm_nt(qt, kt), 0.0)
            b3 = bh.reshape(nblk, SUBLANES, HG_DK)
            q3 = qh.reshape(nblk, SUBLANES, HG_DK)
            k3 = kh.reshape(nblk, SUBLANES, HG_DK)
            v3 = vh.reshape(nblk, SUBLANES, HG_DV)
            od = jnp.zeros((nblk, SUBLANES, HG_DV), F32)
            for s in range(SUBLANES):
                dec = jnp.exp(jnp.minimum(b3 - b3[:, s:s + 1, :], 0.0))
                col = jnp.sum(q3 * k3[:, s:s + 1, :] * dec, axis=-1, keepdims=True)
                od = od + jnp.where(t_local >= s, col, 0.0) * v3[:, s:s + 1, :]
            st = st_ref[h]
            o = _mm(attn, vh) + od.reshape(CHUNK, HG_DV) + _mm_nt(qh * jnp.exp(bh), st)
            blast = bh[CHUNK - 1:CHUNK, :]
            st_ref[h] = st * jnp.exp(blast) + _mm_tn(vh, kh * jnp.exp(blast - bh))
            o = o * lax.rsqrt(jnp.mean(o * o, axis=-1, keepdims=True) + NORM_EPS) * nw
            outs.append(o * _silu(gate[:, sl]))
        o_ref[rows, :] = jnp.concatenate(outs, axis=-1)
        return carry

    lax.fori_loop(0, n_chunks, chunk_body, 0)


def hgrn2(q_lin, f_lin, i_val, g_lin, lb_logits, norm_w, layer, batch, seq):
    t = q_lin.shape[0]
    tl = min(MIX_TILE, seq)
    steps = seq // tl
    blk = pl.BlockSpec((tl, HG_QK), lambda b, j: (b * steps + j, 0))
    n_layers = lb_logits.shape[0]
    return pl.pallas_call(
        functools.partial(_hgrn2_kernel, layer=layer, n_chunks=tl // CHUNK),
        grid=(batch, steps),
        in_specs=[blk, blk, blk, blk, _resident((n_layers, HG_QK)), _resident((1, HG_DV))],
        out_specs=blk,
        out_shape=jax.ShapeDtypeStruct((t, HG_WIDTH), F32),
        scratch_shapes=[pltpu.VMEM((HG_HEADS, HG_DV, HG_DK), F32)],
        compiler_params=_params("parallel", "arbitrary"),
        name="hgrn2",
    )(q_lin, f_lin, i_val, g_lin, lb_logits, norm_w.reshape(1, HG_DV))


def _s5_prep_kernel(are_ref, aim_ref, ldt_ref, bre_ref, bim_ref, abr_ref, abi_ref, bbr_ref, bbi_ref):
    a_re, a_im = are_ref[...], aim_ref[...]
    dt = jnp.exp(ldt_ref[...])
    mag = jnp.exp(dt * a_re)
    ang = dt * a_im
    abar_re = mag * jnp.cos(ang)
    abar_im = mag * jnp.sin(ang)
    den = a_re * a_re + a_im * a_im
    zr = abar_re - 1.0
    zi = abar_im
    coef_re = (zr * a_re + zi * a_im) / den
    coef_im = (zi * a_re - zr * a_im) / den
    b_re, b_im = bre_ref[...], bim_ref[...]
    abr_ref[...] = abar_re
    abi_ref[...] = abar_im
    bbr_ref[...] = coef_re * b_re - coef_im * b_im
    bbi_ref[...] = coef_re * b_im + coef_im * b_re


def s5_prep(a_re, a_im, log_dt, b_re, b_im):
    flat = lambda a: a.reshape(1, S5_NSTATE)
    ldt = jnp.broadcast_to(log_dt[:, None], (S5_GROUPS, S5_STATE))
    by_p = lambda b: b.transpose(2, 0, 1).reshape(S5_GROUP, S5_NSTATE)
    row = jax.ShapeDtypeStruct((1, S5_NSTATE), F32)
    mat = jax.ShapeDtypeStruct((S5_GROUP, S5_NSTATE), F32)
    return pl.pallas_call(_s5_prep_kernel, out_shape=[row, row, mat, mat], name="s5_prep")(
        flat(a_re), flat(a_im), flat(ldt), by_p(b_re), by_p(b_im))


def _s5_kernel(u_ref, abr_ref, abi_ref, bre_ref, bim_ref, cre_ref, cim_ref, d_ref, wglu_ref, o_ref,
               hre_ref, him_ref, sre_ref, sim_ref, y_ref, *, tt, nb):
    @pl.when(pl.program_id(0) == 0)
    def _():
        hre_ref[...] = jnp.zeros(hre_ref.shape, F32)
        him_ref[...] = jnp.zeros(him_ref.shape, F32)

    u = u_ref[...].reshape(tt * nb, S5_WIDTH)
    for j in range(S5_WIDTH // S5_LANE_CHUNK):
        uj = u[:, j * S5_LANE_CHUNK:(j + 1) * S5_LANE_CHUNK].astype(BF16)
        ssl = slice(j * S5_STATE_CHUNK, (j + 1) * S5_STATE_CHUNK)
        sre_ref[...] = jnp.dot(uj, bre_ref[j], preferred_element_type=F32)
        sim_ref[...] = jnp.dot(uj, bim_ref[j], preferred_element_type=F32)
        ar = abr_ref[:, ssl]
        ai = abi_ref[:, ssl]

        def step(t, carry):
            hr, hi = carry
            rows = pl.ds(pl.multiple_of(t * nb, nb), nb)
            nr = ar * hr - ai * hi + sre_ref[rows, :]
            ni = ar * hi + ai * hr + sim_ref[rows, :]
            sre_ref[rows, :] = nr
            sim_ref[rows, :] = ni
            return nr, ni

        hr, hi = lax.fori_loop(0, tt, step, (hre_ref[:, ssl], him_ref[:, ssl]))
        hre_ref[:, ssl] = hr
        him_ref[:, ssl] = hi
        y_ref[:, j * S5_LANE_CHUNK:(j + 1) * S5_LANE_CHUNK] = (
            jnp.dot(sre_ref[...].astype(BF16), cre_ref[j], preferred_element_type=F32)
            - jnp.dot(sim_ref[...].astype(BF16), cim_ref[j], preferred_element_type=F32))
    y = jax.nn.gelu(y_ref[...] + d_ref[...] * u)
    out = y * _sigmoid(jnp.dot(y.astype(BF16), wglu_ref[...], preferred_element_type=F32))
    o_ref[...] = out.reshape(tt, nb, S5_WIDTH)


def _group_block_diag(m):
    per = S5_LANE_CHUNK // S5_GROUP
    g, r, c = m.shape
    m = m.reshape(g // per, per, r, c)
    eye = jnp.eye(per, dtype=m.dtype)
    return jnp.einsum("jgrc,gh->jgrhc", m, eye).reshape(g // per, per * r, per * c)


def s5(u_tm, a_re, a_im, b_re, b_im, c_re, c_im, d, log_dt, w_glu, batch, seq):
    abar_re, abar_im, bb_re, bb_im = s5_prep(a_re, a_im, log_dt, b_re, b_im)
    to_groups = lambda bb: bb.reshape(S5_GROUP, S5_GROUPS, S5_STATE).transpose(1, 0, 2)
    bre = _group_block_diag(to_groups(bb_re)).astype(BF16)
    bim = _group_block_diag(to_groups(bb_im)).astype(BF16)
    cre = _group_block_diag(c_re.transpose(0, 2, 1)).astype(BF16)
    cim = _group_block_diag(c_im.transpose(0, 2, 1)).astype(BF16)
    tt = min(S5_TILE, seq)
    u3 = u_tm.reshape(seq, batch, S5_WIDTH)
    blk = pl.BlockSpec((tt, batch, S5_WIDTH), lambda i: (i, 0, 0))
    rows = tt * batch
    out = pl.pallas_call(
        functools.partial(_s5_kernel, tt=tt, nb=batch),
        grid=(seq // tt,),
        in_specs=[blk, _resident((1, S5_NSTATE)), _resident((1, S5_NSTATE)),
                  _resident(bre.shape), _resident(bim.shape), _resident(cre.shape),
                  _resident(cim.shape), _resident((1, S5_WIDTH)), _resident((S5_WIDTH, S5_WIDTH))],
        out_specs=blk,
        out_shape=jax.ShapeDtypeStruct((seq, batch, S5_WIDTH), F32),
        scratch_shapes=[pltpu.VMEM((batch, S5_NSTATE), F32), pltpu.VMEM((batch, S5_NSTATE), F32),
                        pltpu.VMEM((rows, S5_STATE_CHUNK), F32), pltpu.VMEM((rows, S5_STATE_CHUNK), F32),
                        pltpu.VMEM((rows, S5_WIDTH), F32)],
        compiler_params=_params("arbitrary"),
        name="s5",
    )(u3, abar_re, abar_im, bre, bim, cre, cim, d.reshape(1, S5_WIDTH), w_glu.astype(BF16))
    return out.reshape(seq, batch * S5_WIDTH)


def _softplus(x):
    return jnp.maximum(x, 0.0) + jnp.log(1.0 + jnp.exp(-jnp.abs(x)))


def _gdn_kernel(ql_ref, kl_ref, vl_ref, gate_ref, bd_ref, cw_ref, alog_ref, dtb_ref, nw_ref, o_ref,
                s_ref, tail_ref, pad_ref, *, tl):
    first = pl.program_id(1) == 0

    @pl.when(first)
    def _():
        s_ref[...] = jnp.zeros(s_ref.shape, F32)
        tail_ref[...] = jnp.zeros(tail_ref.shape, F32)

    def conv_silu(x_ref, idx):
        pad_ref[0:SUBLANES, :] = tail_ref[idx]
        pad_ref[SUBLANES:, :] = x_ref[...]
        tail_ref[idx] = x_ref[tl - SUBLANES:tl, :]
        w = cw_ref[idx]
        acc = jnp.zeros((tl, GDN_QK), F32)
        for j in range(CONV_W):
            off = SUBLANES - (CONV_W - 1) + j
            acc = acc + w[j:j + 1, :] * pad_ref[off:off + tl, :]
        return _silu(acc)

    q = conv_silu(ql_ref, 0)
    k = conv_silu(kl_ref, 1)
    v = conv_silu(vl_ref, 2)

    bd = bd_ref[...]
    beta_all = _sigmoid(bd)
    la_all = -jnp.exp(alog_ref[...]) * _softplus(bd + dtb_ref[...])
    rt = _iota2((CHUNK, CHUNK), 0)
    ct = _iota2((CHUNK, CHUNK), 1)
    tri = (rt >= ct).astype(F32)
    lower = rt >= ct
    strict = rt > ct
    eye = (rt == ct).astype(F32)
    nw = nw_ref[...]

    for c in range(tl // CHUNK):
        rows = slice(c * CHUNK, (c + 1) * CHUNK)
        g_all = _mm_exact(tri, la_all[rows, :])
        g_t = g_all.T
        outs = []
        for h in range(GDN_HEADS):
            sl = slice(h * GDN_DK, (h + 1) * GDN_DK)
            qh, kh, vh = q[rows, sl], k[rows, sl], v[rows, sl]
            qh = qh * lax.rsqrt(jnp.sum(qh * qh, axis=-1, keepdims=True) + NORM_EPS) * (GDN_DK ** -0.5)
            kh = kh * lax.rsqrt(jnp.sum(kh * kh, axis=-1, keepdims=True) + NORM_EPS)
            beta = beta_all[rows, h:h + 1]
            gcol = g_all[:, GDN_HEADS + h:GDN_HEADS + h + 1]
            grow = g_t[GDN_HEADS + h:GDN_HEADS + h + 1, :]
            lmask = jnp.where(lower, jnp.exp(jnp.minimum(gcol - grow, 0.0)), 0.0)
            kb = kh * beta
            vb = vh * beta
            m = jnp.where(strict, _mm_nt(kb, kh) * lmask, 0.0)
            n2 = -m
            t_inv = eye + n2
            for _ in range(5):
                n2 = _mm_exact(n2, n2)
                t_inv = t_inv + _mm_exact(t_inv, n2)
            eg = jnp.exp(gcol)
            u_c = _mm(t_inv, vb)
            w_c = _mm(t_inv, kb * eg)
            attn = _mm_nt(qh, kh) * lmask
            s = s_ref[h]
            v_new = u_c - _mm(w_c, s)
            o = _mm(qh * eg, s) + _mm(attn, v_new)
            glast = gcol[CHUNK - 1:CHUNK, :]
            s_ref[h] = s * jnp.exp(glast) + _mm_tn(kh * jnp.exp(glast - gcol), v_new)
            o = o * lax.rsqrt(jnp.mean(o * o, axis=-1, keepdims=True) + NORM_EPS) * nw
            outs.append(o * _silu(gate_ref[rows, sl]))
        o_ref[rows, :] = jnp.concatenate(outs, axis=-1)


def gdn(q_lin, k_lin, v_lin, gate, bd, conv_w, a_log, dt_bias, norm_w, batch, seq):
    t = q_lin.shape[0]
    tl = min(MIX_TILE, seq)
    steps = seq // tl
    wide = pl.BlockSpec((tl, GDN_QK), lambda b, j: (b * steps + j, 0))
    narrow = pl.BlockSpec((tl, LANES), lambda b, j: (b * steps + j, 0))
    cw = conv_w.reshape(CONV_W, 3, GDN_QK).transpose(1, 0, 2)
    lane_row = lambda p: jnp.zeros((1, LANES), F32).at[0, GDN_HEADS:2 * GDN_HEADS].set(p)
    return pl.pallas_call(
        functools.partial(_gdn_kernel, tl=tl),
        grid=(batch, steps),
        in_specs=[wide, wide, wide, wide, narrow, _resident((3, CONV_W, GDN_QK)),
                  _resident((1, LANES)), _resident((1, LANES)), _resident((1, GDN_DV))],
        out_specs=wide,
        out_shape=jax.ShapeDtypeStruct((t, GDN_V), F32),
        scratch_shapes=[pltpu.VMEM((GDN_HEADS, GDN_DK, GDN_DV), F32),
                        pltpu.VMEM((3, SUBLANES, GDN_QK), F32),
                        pltpu.VMEM((tl + SUBLANES, GDN_QK), F32)],
        compiler_params=_params("parallel", "arbitrary"),
        name="gdn",
    )(q_lin, k_lin, v_lin, gate, bd, cw, lane_row(a_log), lane_row(dt_bias),
      norm_w.reshape(1, GDN_DV))


def kernel(x, ffn1_norm, ffn1_w_gate, ffn1_w_up, ffn1_w_down, mix_norm, ffn2_norm, ffn2_w_gate, ffn2_w_up, ffn2_w_down, ev_w_in, hg_lb_logits, hg_norm_w, s5_a_re, s5_a_im, s5_b_re, s5_b_im, s5_c_re, s5_c_im, s5_d, s5_log_dt, s5_w_glu, ev_w_out, od_w_in, gdn_conv_w, gdn_a_log, gdn_dt_bias, gdn_norm_w, od_w_out, final_norm):
    batch, seq, _ = x.shape
    xf = x.reshape(batch * seq, D_MODEL)
    for layer in range(DEPTH):
        xf = ffn(xf, ffn1_norm[layer], ffn1_w_gate[layer], ffn1_w_up[layer], ffn1_w_down[layer])
        j = layer // 2
        if layer % 2 == 0:
            q_lin, f_lin, i_val, g_lin, u_tm = norm_proj(
                xf, mix_norm[layer], ev_w_in[j], (HG_QK, HG_QK, HG_WIDTH, HG_WIDTH, S5_WIDTH), seq,
                time_major_last=True)
            y_a = hgrn2(q_lin, f_lin, i_val, g_lin, hg_lb_logits, hg_norm_w[j], j, batch, seq)
            y_b = s5(u_tm, s5_a_re[j], s5_a_im[j], s5_b_re[j], s5_b_im[j], s5_c_re[j], s5_c_im[j],
                     s5_d[j], s5_log_dt[j], s5_w_glu[j], batch, seq)
            w_out = ev_w_out[j]
            xf = out_proj(xf, [(y_a, w_out[:HG_WIDTH], False), (y_b, w_out[HG_WIDTH:], True)], seq)
        else:
            n_main = 2 * GDN_QK + 2 * GDN_V
            w_in = od_w_in[j]
            w_in = jnp.pad(w_in, ((0, 0), (0, LANES - (w_in.shape[1] - n_main))))
            q_lin, k_lin, v_lin, gate, bd = norm_proj(
                xf, mix_norm[layer], w_in, (GDN_QK, GDN_QK, GDN_V, GDN_V, LANES), seq)
            y = gdn(q_lin, k_lin, v_lin, gate, bd, gdn_conv_w[j], gdn_a_log[j], gdn_dt_bias[j],
                    gdn_norm_w[j], batch, seq)
            xf = out_proj(xf, [(y, od_w_out[j], False)], seq)
        last = layer == DEPTH - 1
        xf = ffn(xf, ffn2_norm[layer], ffn2_w_gate[layer], ffn2_w_up[layer], ffn2_w_down[layer],
                 final_w=final_norm if last else None)
    return xf.reshape(batch, seq, D_MODEL)
```

```python
import functools

import jax
import jax.numpy as jnp
from jax import lax
from jax.experimental import pallas as pl
from jax.experimental.pallas import tpu as pltpu

F32 = jnp.float32
BF16 = jnp.bfloat16
HIGHEST = lax.Precision.HIGHEST

D_MODEL = 1024
DEPTH = 4
D_FF = 2816
NORM_EPS = 1e-6
F_MIN = 1e-6
CHUNK = 64
HG_HEADS = 4
HG_DK = 128
HG_DV = 128
HG_QK = HG_HEADS * HG_DK
HG_WIDTH = HG_HEADS * HG_DV
S5_WIDTH = D_MODEL - HG_WIDTH
S5_GROUP = 16
S5_GROUPS = S5_WIDTH // S5_GROUP
S5_STATE = 64
S5_NSTATE = S5_GROUPS * S5_STATE
GDN_HEADS = 8
GDN_DK = 128
GDN_DV = 128
GDN_QK = GDN_HEADS * GDN_DK
GDN_V = GDN_HEADS * GDN_DV
CONV_W = 4

LANES = 128
SUBLANES = 8
VMEM_LIMIT_BYTES = 56 * 1024 * 1024

TOKEN_TILE = 512
FF_BLOCK = 256
MIX_TILE = 256
S5_TILE = 64
S5_LANE_CHUNK = 128
S5_STATE_CHUNK = S5_LANE_CHUNK // S5_GROUP * S5_STATE


def _mm(a, b):
    return jnp.dot(a.astype(BF16), b.astype(BF16), preferred_element_type=F32)


def _mm_nt(a, b):
    return lax.dot_general(a.astype(BF16), b.astype(BF16), (((1,), (1,)), ((), ())),
                           preferred_element_type=F32)


def _mm_tn(a, b):
    return lax.dot_general(a.astype(BF16), b.astype(BF16), (((0,), (0,)), ((), ())),
                           preferred_element_type=F32)


def _mm_exact(a, b):
    return jnp.dot(a, b, precision=HIGHEST, preferred_element_type=F32)


def _sigmoid(x):
    return 1.0 / (1.0 + jnp.exp(-x))


def _silu(x):
    return x * _sigmoid(x)


def _rmsnorm(x, w):
    return x * lax.rsqrt(jnp.mean(x * x, axis=-1, keepdims=True) + NORM_EPS) * w


def _resident(shape):
    nd = len(shape)
    return pl.BlockSpec(shape, lambda *_: (0,) * nd, pipeline_mode=pl.Buffered(1))


def _params(*sem):
    return pltpu.CompilerParams(dimension_semantics=sem, vmem_limit_bytes=VMEM_LIMIT_BYTES)


def _ffn_kernel(*refs, final):
    if final:
        x_ref, nw_ref, wg_ref, wu_ref, wd_ref, fw_ref, o_ref = refs
    else:
        x_ref, nw_ref, wg_ref, wu_ref, wd_ref, o_ref = refs
    x = x_ref[...]
    h = _rmsnorm(x, nw_ref[...]).astype(BF16)
    acc = jnp.zeros(x.shape, F32)
    for j in range(D_FF // FF_BLOCK):
        sl = slice(j * FF_BLOCK, (j + 1) * FF_BLOCK)
        g = jnp.dot(h, wg_ref[:, sl], preferred_element_type=F32)
        u = jnp.dot(h, wu_ref[:, sl], preferred_element_type=F32)
        a = (_silu(g) * u).astype(BF16)
        acc = acc + jnp.dot(a, wd_ref[sl, :], preferred_element_type=F32)
    y = x + 0.5 * acc
    if final:
        y = _rmsnorm(y, fw_ref[...])
    o_ref[...] = y


def ffn(x, norm_w, w_gate, w_up, w_down, final_w=None):
    t = x.shape[0]
    tm = min(TOKEN_TILE, t)
    final = final_w is not None
    row = pl.BlockSpec((tm, D_MODEL), lambda i: (i, 0))
    in_specs = [row, _resident((1, D_MODEL)), _resident((D_MODEL, D_FF)),
                _resident((D_MODEL, D_FF)), _resident((D_FF, D_MODEL))]
    args = [x, norm_w.reshape(1, D_MODEL), w_gate.astype(BF16), w_up.astype(BF16),
            w_down.astype(BF16)]
    if final:
        in_specs.append(_resident((1, D_MODEL)))
        args.append(final_w.reshape(1, D_MODEL))
    return pl.pallas_call(
        functools.partial(_ffn_kernel, final=final),
        grid=(t // tm,),
        in_specs=in_specs,
        out_specs=row,
        out_shape=jax.ShapeDtypeStruct((t, D_MODEL), F32),
        compiler_params=_params("parallel"),
        name="ffn_final" if final else "ffn",
    )(*args)


def _norm_proj_kernel(x_ref, nw_ref, w_ref, *o_refs, splits):
    h = _rmsnorm(x_ref[...], nw_ref[...]).astype(BF16)
    off = 0
    for o_ref, n in zip(o_refs, splits):
        o_ref[...] = jnp.dot(h, w_ref[:, off:off + n], preferred_element_type=F32)
        off += n


def norm_proj(x, norm_w, w, splits, seq, time_major_last=False):
    t = x.shape[0]
    tm = min(TOKEN_TILE, seq)
    n_total = sum(splits)
    blocks_per_seq = seq // tm
    out_specs, out_shapes = [], []
    for idx, n in enumerate(splits):
        if time_major_last and idx == len(splits) - 1:
            out_specs.append(pl.BlockSpec(
                (tm, n), lambda i: (i % blocks_per_seq, i // blocks_per_seq)))
            out_shapes.append(jax.ShapeDtypeStruct((seq, (t // seq) * n), F32))
        else:
            out_specs.append(pl.BlockSpec((tm, n), lambda i: (i, 0)))
            out_shapes.append(jax.ShapeDtypeStruct((t, n), F32))
    return pl.pallas_call(
        functools.partial(_norm_proj_kernel, splits=splits),
        grid=(t // tm,),
        in_specs=[pl.BlockSpec((tm, D_MODEL), lambda i: (i, 0)), _resident((1, D_MODEL)),
                  _resident((D_MODEL, n_total))],
        out_specs=out_specs,
        out_shape=out_shapes,
        compiler_params=_params("parallel"),
        name="norm_proj",
    )(x, norm_w.reshape(1, D_MODEL), w.astype(BF16))


def _out_proj_kernel(x_ref, *refs):
    o_ref = refs[-1]
    acc = x_ref[...]
    for y_ref, w_ref in zip(refs[0:-1:2], refs[1:-1:2]):
        acc = acc + jnp.dot(y_ref[...].astype(BF16), w_ref[...], preferred_element_type=F32)
    o_ref[...] = acc


def out_proj(x, parts, seq):
    t = x.shape[0]
    tm = min(TOKEN_TILE, seq)
    blocks_per_seq = seq // tm
    row = pl.BlockSpec((tm, D_MODEL), lambda i: (i, 0))
    in_specs, args = [row], [x]
    for y, w, time_major in parts:
        n = w.shape[0]
        if time_major:
            in_specs.append(pl.BlockSpec(
                (tm, n), lambda i: (i % blocks_per_seq, i // blocks_per_seq)))
        else:
            in_specs.append(pl.BlockSpec((tm, n), lambda i: (i, 0)))
        in_specs.append(_resident((n, D_MODEL)))
        args += [y, w.astype(BF16)]
    return pl.pallas_call(
        _out_proj_kernel,
        grid=(t // tm,),
        in_specs=in_specs,
        out_specs=row,
        out_shape=jax.ShapeDtypeStruct((t, D_MODEL), F32),
        compiler_params=_params("parallel"),
        name="out_proj",
    )(*args)


def _iota2(shape, axis):
    return lax.broadcasted_iota(jnp.int32, shape, axis)


def _hgrn2_kernel(q_ref, f_ref, i_ref, g_ref, lbl_ref, nw_ref, o_ref, st_ref, *, layer, n_chunks):
    @pl.when(pl.program_id(1) == 0)
    def _():
        st_ref[...] = jnp.zeros(st_ref.shape, F32)

    logits = lbl_ref[...]
    e = jnp.exp(logits - jnp.max(logits, axis=0, keepdims=True))
    p = e / jnp.sum(e, axis=0, keepdims=True)
    lb = jnp.sum(p[0:layer + 1], axis=0, keepdims=True) - p[0:1]
    nw = nw_ref[...]

    rt = _iota2((CHUNK, CHUNK), 0)
    ct = _iota2((CHUNK, CHUNK), 1)
    tri = (rt >= ct).astype(F32)
    levels = (32, 16, 8)
    level_mask = {m: ((rt // m) % 2 == 1) & (ct // m == rt // m - 1) for m in levels}
    nblk = CHUNK // SUBLANES
    t_local = _iota2((nblk, SUBLANES, 1), 1)

    def chunk_body(c, carry):
        rows = pl.ds(pl.multiple_of(c * CHUNK, CHUNK), CHUNK)
        q = _silu(q_ref[rows, :])
        f = lb + (1.0 - lb) * _sigmoid(f_ref[rows, :])
        lf = jnp.log(jnp.maximum(f, F_MIN))
        kk = 1.0 - f
        v = i_ref[rows, :]
        gate = g_ref[rows, :]
        b = _mm_exact(tri, lf)
        outs = []
        for h in range(HG_HEADS):
            sl = slice(h * HG_DK, (h + 1) * HG_DK)
            bh, qh, kh, vh = b[:, sl], q[:, sl], kk[:, sl], v[:, sl]
            attn = jnp.zeros((CHUNK, CHUNK), F32)
            for m in levels:
                nb = CHUNK // m
                b3 = bh.reshape(nb, m, HG_DK)
                bend = b3[:, m - 1:m, :]
                bprev = jnp.concatenate([jnp.zeros((1, 1, HG_DK), F32), bend[:-1]], axis=0)
                qt = qh * jnp.exp(b3 - bprev).reshape(CHUNK, HG_DK)
                kt = kh * jnp.exp(bend - b3).reshape(CHUNK, HG_DK)
                attn = attn + jnp.where(level_mask[m], _mm_nt(qt, kt), 0.0)
            b3 = bh.reshape(nblk, SUBLANES, HG_DK)
            q3 = qh.reshape(nblk, SUBLANES, HG_DK)
            k3 = kh.reshape(nblk, SUBLANES, HG_DK)
            v3 = vh.reshape(nblk, SUBLANES, HG_DV)
            od = jnp.zeros((nblk, SUBLANES, HG_DV), F32)
            for s in range(SUBLANES):
                dec = jnp.exp(jnp.minimum(b3 - b3[:, s:s + 1, :], 0.0))
                col = jnp.sum(q3 * k3[:, s:s + 1, :] * dec, axis=-1, keepdims=True)
                od = od + jnp.where(t_local >= s, col, 0.0) * v3[:, s:s + 1, :]
            st = st_ref[h]
            o = _mm(attn, vh) + od.reshape(CHUNK, HG_DV) + _mm_nt(qh * jnp.exp(bh), st)
            blast = bh[CHUNK - 1:CHUNK, :]
            st_ref[h] = st * jnp.exp(blast) + _mm_tn(vh, kh * jnp.exp(blast - bh))
            o = o * lax.rsqrt(jnp.mean(o * o, axis=-1, keepdims=True) + NORM_EPS) * nw
            outs.append(o * _silu(gate[:, sl]))
        o_ref[rows, :] = jnp.concatenate(outs, axis=-1)
        return carry

    lax.fori_loop(0, n_chunks, chunk_body, 0)


def hgrn2(q_lin, f_lin, i_val, g_lin, lb_logits, norm_w, layer, batch, seq):
    t = q_lin.shape[0]
    tl = min(MIX_TILE, seq)
    steps = seq // tl
    blk = pl.BlockSpec((tl, HG_QK), lambda b, j: (b * steps + j, 0))
    n_layers = lb_logits.shape[0]
    return pl.pallas_call(
        functools.partial(_hgrn2_kernel, layer=layer, n_chunks=tl // CHUNK),
        grid=(batch, steps),
        in_specs=[blk, blk, blk, blk, _resident((n_layers, HG_QK)), _resident((1, HG_DV))],
        out_specs=blk,
        out_shape=jax.ShapeDtypeStruct((t, HG_WIDTH), F32),
        scratch_shapes=[pltpu.VMEM((HG_HEADS, HG_DV, HG_DK), F32)],
        compiler_params=_params("parallel", "arbitrary"),
        name="hgrn2",
    )(q_lin, f_lin, i_val, g_lin, lb_logits, norm_w.reshape(1, HG_DV))


def _s5_prep_kernel(are_ref, aim_ref, ldt_ref, bre_ref, bim_ref, abr_ref, abi_ref, bbr_ref, bbi_ref):
    a_re, a_im = are_ref[...], aim_ref[...]
    dt = jnp.exp(ldt_ref[...])
    mag = jnp.exp(dt * a_re)
    ang = dt * a_im
    abar_re = mag * jnp.cos(ang)
    abar_im = mag * jnp.sin(ang)
    den = a_re * a_re + a_im * a_im
    zr = abar_re - 1.0
    zi = abar_im
    coef_re = (zr * a_re + zi * a_im) / den
    coef_im = (zi * a_re - zr * a_im) / den
    b_re, b_im = bre_ref[...], bim_ref[...]
    abr_ref[...] = abar_re
    abi_ref[...] = abar_im
    bbr_ref[...] = coef_re * b_re - coef_im * b_im
    bbi_ref[...] = coef_re * b_im + coef_im * b_re


def s5_prep(a_re, a_im, log_dt, b_re, b_im):
    flat = lambda a: a.reshape(1, S5_NSTATE)
    ldt = jnp.broadcast_to(log_dt[:, None], (S5_GROUPS, S5_STATE))
    by_p = lambda b: b.transpose(2, 0, 1).reshape(S5_GROUP, S5_NSTATE)
    row = jax.ShapeDtypeStruct((1, S5_NSTATE), F32)
    mat = jax.ShapeDtypeStruct((S5_GROUP, S5_NSTATE), F32)
    return pl.pallas_call(_s5_prep_kernel, out_shape=[row, row, mat, mat], name="s5_prep")(
        flat(a_re), flat(a_im), flat(ldt), by_p(b_re), by_p(b_im))


def _s5_kernel(u_ref, abr_ref, abi_ref, bre_ref, bim_ref, cre_ref, cim_ref, d_ref, wglu_ref, o_ref,
               hre_ref, him_ref, sre_ref, sim_ref, y_ref, *, tt, nb):
    @pl.when(pl.program_id(0) == 0)
    def _():
        hre_ref[...] = jnp.zeros(hre_ref.shape, F32)
        him_ref[...] = jnp.zeros(him_ref.shape, F32)

    u = u_ref[...].reshape(tt * nb, S5_WIDTH)
    for j in range(S5_WIDTH // S5_LANE_CHUNK):
        uj = u[:, j * S5_LANE_CHUNK:(j + 1) * S5_LANE_CHUNK].astype(BF16)
        ssl = slice(j * S5_STATE_CHUNK, (j + 1) * S5_STATE_CHUNK)
        sre_ref[...] = jnp.dot(uj, bre_ref[j], preferred_element_type=F32)
        sim_ref[...] = jnp.dot(uj, bim_ref[j], preferred_element_type=F32)
        ar = abr_ref[:, ssl]
        ai = abi_ref[:, ssl]

        def step(t, carry):
            hr, hi = carry
            rows = pl.ds(pl.multiple_of(t * nb, nb), nb)
            nr = ar * hr - ai * hi + sre_ref[rows, :]
            ni = ar * hi + ai * hr + sim_ref[rows, :]
            sre_ref[rows, :] = nr
            sim_ref[rows, :] = ni
            return nr, ni

        hr, hi = lax.fori_loop(0, tt, step, (hre_ref[:, ssl], him_ref[:, ssl]))
        hre_ref[:, ssl] = hr
        him_ref[:, ssl] = hi
        y_ref[:, j * S5_LANE_CHUNK:(j + 1) * S5_LANE_CHUNK] = (
            jnp.dot(sre_ref[...].astype(BF16), cre_ref[j], preferred_element_type=F32)
            - jnp.dot(sim_ref[...].astype(BF16), cim_ref[j], preferred_element_type=F32))
    y = jax.nn.gelu(y_ref[...] + d_ref[...] * u)
    out = y * _sigmoid(jnp.dot(y.astype(BF16), wglu_ref[...], preferred_element_type=F32))
    o_ref[...] = out.reshape(tt, nb, S5_WIDTH)


def _group_block_diag(m):
    per = S5_LANE_CHUNK // S5_GROUP
    g, r, c = m.shape
    m = m.reshape(g // per, per, r, c)
    eye = jnp.eye(per, dtype=m.dtype)
    return jnp.einsum("jgrc,gh->jgrhc", m, eye).reshape(g // per, per * r, per * c)


def s5(u_tm, a_re, a_im, b_re, b_im, c_re, c_im, d, log_dt, w_glu, batch, seq):
    abar_re, abar_im, bb_re, bb_im = s5_prep(a_re, a_im, log_dt, b_re, b_im)
    to_groups = lambda bb: bb.reshape(S5_GROUP, S5_GROUPS, S5_STATE).transpose(1, 0, 2)
    bre = _group_block_diag(to_groups(bb_re)).astype(BF16)
    bim = _group_block_diag(to_groups(bb_im)).astype(BF16)
    cre = _group_block_diag(c_re.transpose(0, 2, 1)).astype(BF16)
    cim = _group_block_diag(c_im.transpose(0, 2, 1)).astype(BF16)
    tt = min(S5_TILE, seq)
    u3 = u_tm.reshape(seq, batch, S5_WIDTH)
    blk = pl.BlockSpec((tt, batch, S5_WIDTH), lambda i: (i, 0, 0))
    rows = tt * batch
    out = pl.pallas_call(
        functools.partial(_s5_kernel, tt=tt, nb=batch),
        grid=(seq // tt,),
        in_specs=[blk, _resident((1, S5_NSTATE)), _resident((1, S5_NSTATE)),
                  _resident(bre.shape), _resident(bim.shape), _resident(cre.shape),
                  _resident(cim.shape), _resident((1, S5_WIDTH)), _resident((S5_WIDTH, S5_WIDTH))],
        out_specs=blk,
        out_shape=jax.ShapeDtypeStruct((seq, batch, S5_WIDTH), F32),
        scratch_shapes=[pltpu.VMEM((batch, S5_NSTATE), F32), pltpu.VMEM((batch, S5_NSTATE), F32),
                        pltpu.VMEM((rows, S5_STATE_CHUNK), F32), pltpu.VMEM((rows, S5_STATE_CHUNK), F32),
                        pltpu.VMEM((rows, S5_WIDTH), F32)],
        compiler_params=_params("arbitrary"),
        name="s5",
    )(u3, abar_re, abar_im, bre, bim, cre, cim, d.reshape(1, S5_WIDTH), w_glu.astype(BF16))
    return out.reshape(seq, batch * S5_WIDTH)


def _softplus(x):
    return jnp.maximum(x, 0.0) + jnp.log(1.0 + jnp.exp(-jnp.abs(x)))


def _split2(x):
    hi = x.astype(BF16)
    return hi, (x - hi.astype(F32)).astype(BF16)


def _mm3(a, b):
    ah, al = _split2(a)
    bh, bl = _split2(b)
    return jnp.dot(jnp.concatenate([ah, al, ah], axis=1), jnp.concatenate([bh, bh, bl], axis=0),
                   preferred_element_type=F32)


def _mm_sel(sel, x):
    x1 = x.astype(BF16)
    r1 = x - x1.astype(F32)
    x2 = r1.astype(BF16)
    x3 = (r1 - x2.astype(F32)).astype(BF16)
    s = sel.astype(BF16)
    return jnp.dot(jnp.concatenate([s, s, s], axis=1), jnp.concatenate([x1, x2, x3], axis=0),
                   preferred_element_type=F32)


def _gdn_kernel(ql_ref, kl_ref, vl_ref, gate_ref, bd_ref, cw_ref, alog_ref, dtb_ref, nw_ref, o_ref,
                s_ref, tail_ref, *, tl):
    nc = tl // CHUNK

    @pl.when(pl.program_id(1) == 0)
    def _():
        s_ref[...] = jnp.zeros(s_ref.shape, F32)
        tail_ref[...] = jnp.zeros(tail_ref.shape, F32)

    row_id = _iota2((tl, 1), 0)

    def shift_rows(z, carry_row):
        return jnp.where(row_id == 0, carry_row, pltpu.roll(z, 1, 0))

    def conv_silu(x_ref, idx):
        x = x_ref[...]
        w = cw_ref[idx]
        t = tail_ref[idx]
        tail_ref[idx] = x[tl - SUBLANES:tl, :]
        xm1, xm2, xm3 = (t[SUBLANES - i:SUBLANES - i + 1] for i in (1, 2, 3))
        w0, w1, w2, w3 = (w[j:j + 1] for j in range(CONV_W))
        z = w0 * x
        z = w1 * x + shift_rows(z, w0 * xm1)
        z = w2 * x + shift_rows(z, w1 * xm1 + w0 * xm2)
        z = w3 * x + shift_rows(z, w2 * xm1 + w1 * xm2 + w0 * xm3)
        return _silu(z)

    q = conv_silu(ql_ref, 0)
    k = conv_silu(kl_ref, 1)
    v = conv_silu(vl_ref, 2)

    rt = _iota2((tl, tl), 0)
    ct = _iota2((tl, tl), 1)
    same = (rt // CHUNK) == (ct // CHUNK)
    lower = same & (rt >= ct)
    strict = same & (rt > ct)
    rs = _iota2((CHUNK, tl), 0)
    cs = _iota2((CHUNK, tl), 1)
    eye_side = (rs == cs % CHUNK).astype(F32)
    side_block = cs // CHUNK

    def to_diag(side):
        return jnp.where(same, jnp.concatenate([side] * nc, axis=0), 0.0)

    def from_diag(full):
        out = jnp.zeros((CHUNK, tl), F32)
        for c in range(nc):
            out = out + jnp.where(side_block == c, full[c * CHUNK:(c + 1) * CHUNK, :], 0.0)
        return out

    bd = bd_ref[...]
    beta_all = _sigmoid(bd)
    la_all = -jnp.exp(alog_ref[...]) * _softplus(bd + dtb_ref[...])
    g_all = _mm_sel(lower, la_all)
    g_t = g_all.T
    nw = nw_ref[...]

    heads = range(GDN_HEADS)
    kn, qg, vkb, attn, gcols, n_side, t_side = [], [], [], [], [], [], []
    for h in heads:
        sl = slice(h * GDN_DK, (h + 1) * GDN_DK)
        qh, kh, vh = q[:, sl], k[:, sl], v[:, sl]
        qh = qh * lax.rsqrt(jnp.sum(qh * qh, axis=-1, keepdims=True) + NORM_EPS) * (GDN_DK ** -0.5)
        kh = kh * lax.rsqrt(jnp.sum(kh * kh, axis=-1, keepdims=True) + NORM_EPS)
        beta = beta_all[:, h:h + 1]
        gcol = g_all[:, GDN_HEADS + h:GDN_HEADS + h + 1]
        grow = g_t[GDN_HEADS + h:GDN_HEADS + h + 1, :]
        lmask = jnp.where(lower, jnp.exp(jnp.minimum(gcol - grow, 0.0)), 0.0)
        kb = kh * beta
        kq = _mm_nt(jnp.concatenate([kb, qh], axis=0), kh)
        m = jnp.where(strict, kq[:tl] * lmask, 0.0)
        attn.append(kq[tl:] * lmask)
        eg = jnp.exp(gcol)
        n = -from_diag(m)
        n_side.append(n)
        t_side.append(eye_side + n)
        vkb.append(jnp.concatenate([vh * beta, kb * eg], axis=1))
        kn.append(kh)
        qg.append(qh * eg)
        gcols.append(gcol)

    for h in heads:
        n_side[h] = _mm(n_side[h], to_diag(n_side[h]))
    for _ in range(4):
        for h in heads:
            r = _mm(jnp.concatenate([t_side[h], n_side[h]], axis=0), to_diag(n_side[h]))
            t_side[h] = t_side[h] + r[:CHUNK]
            n_side[h] = r[CHUNK:]
    uw = []
    for h in heads:
        t_inv = t_side[h] + _mm(t_side[h], to_diag(n_side[h]))
        uw.append(_mm(to_diag(t_inv), vkb[h]))

    s = [s_ref[h] for h in heads]
    v_new = [[] for _ in heads]
    o_state = [[] for _ in heads]
    for c in range(nc):
        rows = slice(c * CHUNK, (c + 1) * CHUNK)
        ws = [_mm(jnp.concatenate([uw[h][rows, GDN_DV:], qg[h][rows]], axis=0), s[h]) for h in heads]
        for h in heads:
            vn = uw[h][rows, :GDN_DV] - ws[h][:CHUNK]
            v_new[h].append(vn)
            o_state[h].append(ws[h][CHUNK:])
            gc = gcols[h][rows]
            glast = gc[CHUNK - 1:CHUNK, :]
            s[h] = s[h] * jnp.exp(glast) + _mm_tn(kn[h][rows] * jnp.exp(glast - gc), vn)
    for h in heads:
        s_ref[h] = s[h]
        sl = slice(h * GDN_DK, (h + 1) * GDN_DK)
        o = jnp.concatenate(o_state[h], axis=0) + _mm(attn[h], jnp.concatenate(v_new[h], axis=0))
        o = o * lax.rsqrt(jnp.mean(o * o, axis=-1, keepdims=True) + NORM_EPS) * nw
        o_ref[:, sl] = o * _silu(gate_ref[:, sl])


def gdn(q_lin, k_lin, v_lin, gate, bd, conv_w, a_log, dt_bias, norm_w, batch, seq):
    t = q_lin.shape[0]
    tl = min(MIX_TILE, seq)
    steps = seq // tl
    wide = pl.BlockSpec((tl, GDN_QK), lambda b, j: (b * steps + j, 0))
    narrow = pl.BlockSpec((tl, LANES), lambda b, j: (b * steps + j, 0))
    cw = conv_w.reshape(CONV_W, 3, GDN_QK).transpose(1, 0, 2)
    lane_row = lambda p: jnp.zeros((1, LANES), F32).at[0, GDN_HEADS:2 * GDN_HEADS].set(p)
    return pl.pallas_call(
        functools.partial(_gdn_kernel, tl=tl),
        grid=(batch, steps),
        in_specs=[wide, wide, wide, wide, narrow, _resident((3, CONV_W, GDN_QK)),
                  _resident((1, LANES)), _resident((1, LANES)), _resident((1, GDN_DV))],
        out_specs=wide,
        out_shape=jax.ShapeDtypeStruct((t, GDN_V), F32),
        scratch_shapes=[pltpu.VMEM((GDN_HEADS, GDN_DK, GDN_DV), F32),
                        pltpu.VMEM((3, SUBLANES, GDN_QK), F32)],
        compiler_params=_params("parallel", "arbitrary"),
        name="gdn",
    )(q_lin, k_lin, v_lin, gate, bd, cw, lane_row(a_log), lane_row(dt_bias),
      norm_w.reshape(1, GDN_DV))


def kernel(x, ffn1_norm, ffn1_w_gate, ffn1_w_up, ffn1_w_down, mix_norm, ffn2_norm, ffn2_w_gate, ffn2_w_up, ffn2_w_down, ev_w_in, hg_lb_logits, hg_norm_w, s5_a_re, s5_a_im, s5_b_re, s5_b_im, s5_c_re, s5_c_im, s5_d, s5_log_dt, s5_w_glu, ev_w_out, od_w_in, gdn_conv_w, gdn_a_log, gdn_dt_bias, gdn_norm_w, od_w_out, final_norm):
    batch, seq, _ = x.shape
    xf = x.reshape(batch * seq, D_MODEL)
    for layer in range(DEPTH):
        xf = ffn(xf, ffn1_norm[layer], ffn1_w_gate[layer], ffn1_w_up[layer], ffn1_w_down[layer])
        j = layer // 2
        if layer % 2 == 0:
            q_lin, f_lin, i_val, g_lin, u_tm = norm_proj(
                xf, mix_norm[layer], ev_w_in[j], (HG_QK, HG_QK, HG_WIDTH, HG_WIDTH, S5_WIDTH), seq,
                time_major_last=True)
            y_a = hgrn2(q_lin, f_lin, i_val, g_lin, hg_lb_logits, hg_norm_w[j], j, batch, seq)
            y_b = s5(u_tm, s5_a_re[j], s5_a_im[j], s5_b_re[j], s5_b_im[j], s5_c_re[j], s5_c_im[j],
                     s5_d[j], s5_log_dt[j], s5_w_glu[j], batch, seq)
            w_out = ev_w_out[j]
            xf = out_proj(xf, [(y_a, w_out[:HG_WIDTH], False), (y_b, w_out[HG_WIDTH:], True)], seq)
        else:
            n_main = 2 * GDN_QK + 2 * GDN_V
            w_in = od_w_in[j]
            w_in = jnp.pad(w_in, ((0, 0), (0, LANES - (w_in.shape[1] - n_main))))
            q_lin, k_lin, v_lin, gate, bd = norm_proj(
                xf, mix_norm[layer], w_in, (GDN_QK, GDN_QK, GDN_V, GDN_V, LANES), seq)
            y = gdn(q_lin, k_lin, v_lin, gate, bd, gdn_conv_w[j], gdn_a_log[j], gdn_dt_bias[j],
                    gdn_norm_w[j], batch, seq)
            xf = out_proj(xf, [(y, od_w_out[j], False)], seq)
        last = layer == DEPTH - 1
        xf = ffn(xf, ffn2_norm[layer], ffn2_w_gate[layer], ffn2_w_up[layer], ffn2_w_down[layer],
                 final_w=final_norm if last else None)
    return xf.reshape(batch, seq, D_MODEL)
```

```python
import functools

import jax
import jax.numpy as jnp
from jax import lax
from jax.experimental import pallas as pl
from jax.experimental.pallas import tpu as pltpu

F32 = jnp.float32
BF16 = jnp.bfloat16
HIGHEST = lax.Precision.HIGHEST

D_MODEL = 1024
DEPTH = 4
D_FF = 2816
NORM_EPS = 1e-6
F_MIN = 1e-6
CHUNK = 64
HG_HEADS = 4
HG_DK = 128
HG_DV = 128
HG_QK = HG_HEADS * HG_DK
HG_WIDTH = HG_HEADS * HG_DV
S5_WIDTH = D_MODEL - HG_WIDTH
S5_GROUP = 16
S5_GROUPS = S5_WIDTH // S5_GROUP
S5_STATE = 64
S5_NSTATE = S5_GROUPS * S5_STATE
GDN_HEADS = 8
GDN_DK = 128
GDN_DV = 128
GDN_QK = GDN_HEADS * GDN_DK
GDN_V = GDN_HEADS * GDN_DV
CONV_W = 4

LANES = 128
SUBLANES = 8
VMEM_LIMIT_BYTES = 56 * 1024 * 1024

TOKEN_TILE = 512
FF_BLOCK = 256
MIX_TILE = 256
S5_TILE = 64
S5_LANE_CHUNK = 128
S5_STATE_CHUNK = S5_LANE_CHUNK // S5_GROUP * S5_STATE


def _mm(a, b):
    return jnp.dot(a.astype(BF16), b.astype(BF16), preferred_element_type=F32)


def _mm_nt(a, b):
    return lax.dot_general(a.astype(BF16), b.astype(BF16), (((1,), (1,)), ((), ())),
                           preferred_element_type=F32)


def _mm_tn(a, b):
    return lax.dot_general(a.astype(BF16), b.astype(BF16), (((0,), (0,)), ((), ())),
                           preferred_element_type=F32)


def _mm_exact(a, b):
    return jnp.dot(a, b, precision=HIGHEST, preferred_element_type=F32)


def _sigmoid(x):
    return 1.0 / (1.0 + jnp.exp(-x))


def _silu(x):
    return x * _sigmoid(x)


def _rmsnorm(x, w):
    return x * lax.rsqrt(jnp.mean(x * x, axis=-1, keepdims=True) + NORM_EPS) * w


def _resident(shape):
    nd = len(shape)
    return pl.BlockSpec(shape, lambda *_: (0,) * nd, pipeline_mode=pl.Buffered(1))


def _params(*sem):
    return pltpu.CompilerParams(dimension_semantics=sem, vmem_limit_bytes=VMEM_LIMIT_BYTES)


def _ffn_kernel(*refs, n_parts, final):
    x_ref, refs = refs[0], refs[1:]
    part_refs, refs = refs[:2 * n_parts], refs[2 * n_parts:]
    if final:
        nw_ref, wg_ref, wu_ref, wd_ref, fw_ref, o_ref = refs
    else:
        nw_ref, wg_ref, wu_ref, wd_ref, o_ref = refs
    x = x_ref[...]
    for y_ref, w_ref in zip(part_refs[0::2], part_refs[1::2]):
        x = x + jnp.dot(y_ref[...].astype(BF16), w_ref[...], preferred_element_type=F32)
    h = _rmsnorm(x, nw_ref[...]).astype(BF16)
    acc = jnp.zeros(x.shape, F32)
    for j in range(D_FF // FF_BLOCK):
        sl = slice(j * FF_BLOCK, (j + 1) * FF_BLOCK)
        g = jnp.dot(h, wg_ref[:, sl], preferred_element_type=F32)
        u = jnp.dot(h, wu_ref[:, sl], preferred_element_type=F32)
        a = (_silu(g) * u).astype(BF16)
        acc = acc + jnp.dot(a, wd_ref[sl, :], preferred_element_type=F32)
    y = x + 0.5 * acc
    if final:
        y = _rmsnorm(y, fw_ref[...])
    o_ref[...] = y


def ffn(x, norm_w, w_gate, w_up, w_down, seq, parts=(), final_w=None):
    t = x.shape[0]
    tm = min(TOKEN_TILE, seq)
    blocks_per_seq = seq // tm
    final = final_w is not None
    row = pl.BlockSpec((tm, D_MODEL), lambda i: (i, 0))
    in_specs, args = [row], [x]
    for y, w, time_major in parts:
        n = w.shape[0]
        if time_major:
            in_specs.append(pl.BlockSpec(
                (tm, n), lambda i: (i % blocks_per_seq, i // blocks_per_seq)))
        else:
            in_specs.append(pl.BlockSpec((tm, n), lambda i: (i, 0)))
        in_specs.append(_resident((n, D_MODEL)))
        args += [y, w.astype(BF16)]
    in_specs += [_resident((1, D_MODEL)), _resident((D_MODEL, D_FF)),
                 _resident((D_MODEL, D_FF)), _resident((D_FF, D_MODEL))]
    args += [norm_w.reshape(1, D_MODEL), w_gate.astype(BF16), w_up.astype(BF16),
             w_down.astype(BF16)]
    if final:
        in_specs.append(_resident((1, D_MODEL)))
        args.append(final_w.reshape(1, D_MODEL))
    return pl.pallas_call(
        functools.partial(_ffn_kernel, n_parts=len(parts), final=final),
        grid=(t // tm,),
        in_specs=in_specs,
        out_specs=row,
        out_shape=jax.ShapeDtypeStruct((t, D_MODEL), F32),
        compiler_params=_params("parallel"),
        name=("ffn_mix" if parts else "ffn") + ("_final" if final else ""),
    )(*args)


def _norm_proj_kernel(x_ref, nw_ref, w_ref, *o_refs, splits):
    h = _rmsnorm(x_ref[...], nw_ref[...]).astype(BF16)
    off = 0
    for o_ref, n in zip(o_refs, splits):
        o_ref[...] = jnp.dot(h, w_ref[:, off:off + n], preferred_element_type=F32)
        off += n


def norm_proj(x, norm_w, w, splits, seq, time_major_last=False):
    t = x.shape[0]
    tm = min(TOKEN_TILE, seq)
    n_total = sum(splits)
    blocks_per_seq = seq // tm
    out_specs, out_shapes = [], []
    for idx, n in enumerate(splits):
        if time_major_last and idx == len(splits) - 1:
            out_specs.append(pl.BlockSpec(
                (tm, n), lambda i: (i % blocks_per_seq, i // blocks_per_seq)))
            out_shapes.append(jax.ShapeDtypeStruct((seq, (t // seq) * n), F32))
        else:
            out_specs.append(pl.BlockSpec((tm, n), lambda i: (i, 0)))
            out_shapes.append(jax.ShapeDtypeStruct((t, n), F32))
    return pl.pallas_call(
        functools.partial(_norm_proj_kernel, splits=splits),
        grid=(t // tm,),
        in_specs=[pl.BlockSpec((tm, D_MODEL), lambda i: (i, 0)), _resident((1, D_MODEL)),
                  _resident((D_MODEL, n_total))],
        out_specs=out_specs,
        out_shape=out_shapes,
        compiler_params=_params("parallel"),
        name="norm_proj",
    )(x, norm_w.reshape(1, D_MODEL), w.astype(BF16))


def _iota2(shape, axis):
    return lax.broadcasted_iota(jnp.int32, shape, axis)


def _hgrn2_kernel(q_ref, f_ref, i_ref, g_ref, lbl_ref, nw_ref, o_ref, st_ref, *, layer, tl):
    nc = tl // CHUNK

    @pl.when(pl.program_id(1) == 0)
    def _():
        st_ref[...] = jnp.zeros(st_ref.shape, F32)

    logits = lbl_ref[...]
    e = jnp.exp(logits - jnp.max(logits, axis=0, keepdims=True))
    p = e / jnp.sum(e, axis=0, keepdims=True)
    lb = jnp.sum(p[0:layer + 1], axis=0, keepdims=True) - p[0:1]
    nw = nw_ref[...]

    q = _silu(q_ref[...])
    f = lb + (1.0 - lb) * _sigmoid(f_ref[...])
    lf = jnp.log(jnp.maximum(f, F_MIN))
    kk = 1.0 - f
    v = i_ref[...]

    rt = _iota2((tl, tl), 0)
    ct = _iota2((tl, tl), 1)
    b = _mm_sel(((rt // CHUNK) == (ct // CHUNK)) & (rt >= ct), lf)
    levels = (32, 16, 8, 4)
    level_mask = {m: ((rt // (2 * m)) == (ct // (2 * m))) & ((rt // m) % 2 == 1) & ((ct // m) % 2 == 0)
                  for m in levels}
    base = 4
    base_mask = ((rt // base) == (ct // base)) & (rt >= ct)
    pos = _iota2((tl, 1), 0) % base
    c4 = lf
    for j in range(1, base):
        c4 = c4 + jnp.where(pos >= j, pltpu.roll(lf, j, 0), 0.0)

    outs = []
    for h in range(HG_HEADS):
        sl = slice(h * HG_DK, (h + 1) * HG_DK)
        bh, qh, kh, vh, ch = b[:, sl], q[:, sl], kk[:, sl], v[:, sl], c4[:, sl]
        attn = jnp.where(base_mask, _mm_nt(qh * jnp.exp(ch), kh * jnp.exp(-ch)), 0.0)
        for m in levels:
            if m == base:
                b3 = bh.reshape(tl // (2 * m), 2 * m, HG_DK)
                ref_q = ref_k = b3[:, m - 1:m, :]
            else:
                b3 = bh.reshape(tl // m, m, HG_DK)
                ref_k = b3[:, m - 1:m, :]
                ref_q = jnp.concatenate([jnp.zeros((1, 1, HG_DK), F32), ref_k[:-1]], axis=0)
            qt = qh * jnp.exp(jnp.minimum(b3 - ref_q, 0.0)).reshape(tl, HG_DK)
            kt = kh * jnp.exp(jnp.minimum(ref_k - b3, 0.0)).reshape(tl, HG_DK)
            attn = jnp.where(level_mask[m], _mm_nt(qt, kt), attn)
        o = _mm(attn, vh)
        bc = bh.reshape(nc, CHUNK, HG_DK)
        blast = bc[:, CHUNK - 1:CHUNK, :]
        ks = kh * jnp.exp(blast - bc).reshape(tl, HG_DK)
        qe = qh * jnp.exp(bh)
        st = st_ref[h]
        o_state = []
        for c in range(nc):
            rows = slice(c * CHUNK, (c + 1) * CHUNK)
            o_state.append(_mm_nt(qe[rows], st))
            st = st * jnp.exp(blast[c]) + _mm_tn(vh[rows], ks[rows])
        st_ref[h] = st
        o = o + jnp.concatenate(o_state, axis=0)
        o = o * lax.rsqrt(jnp.mean(o * o, axis=-1, keepdims=True) + NORM_EPS) * nw
        outs.append(o * _silu(g_ref[:, sl]))
    o_ref[...] = jnp.concatenate(outs, axis=-1)


def hgrn2(q_lin, f_lin, i_val, g_lin, lb_logits, norm_w, layer, batch, seq):
    t = q_lin.shape[0]
    tl = min(MIX_TILE, seq)
    steps = seq // tl
    blk = pl.BlockSpec((tl, HG_QK), lambda b, j: (b * steps + j, 0))
    n_layers = lb_logits.shape[0]
    return pl.pallas_call(
        functools.partial(_hgrn2_kernel, layer=layer, tl=tl),
        grid=(batch, steps),
        in_specs=[blk, blk, blk, blk, _resident((n_layers, HG_QK)), _resident((1, HG_DV))],
        out_specs=blk,
        out_shape=jax.ShapeDtypeStruct((t, HG_WIDTH), F32),
        scratch_shapes=[pltpu.VMEM((HG_HEADS, HG_DV, HG_DK), F32)],
        compiler_params=_params("parallel", "arbitrary"),
        name="hgrn2",
    )(q_lin, f_lin, i_val, g_lin, lb_logits, norm_w.reshape(1, HG_DV))


def _s5_prep_kernel(are_ref, aim_ref, ldt_ref, bre_ref, bim_ref, abr_ref, abi_ref, bbr_ref, bbi_ref):
    a_re, a_im = are_ref[...], aim_ref[...]
    dt = jnp.exp(ldt_ref[...])
    mag = jnp.exp(dt * a_re)
    ang = dt * a_im
    abar_re = mag * jnp.cos(ang)
    abar_im = mag * jnp.sin(ang)
    den = a_re * a_re + a_im * a_im
    zr = abar_re - 1.0
    zi = abar_im
    coef_re = (zr * a_re + zi * a_im) / den
    coef_im = (zi * a_re - zr * a_im) / den
    b_re, b_im = bre_ref[...], bim_ref[...]
    abr_ref[...] = abar_re
    abi_ref[...] = abar_im
    bbr_ref[...] = coef_re * b_re - coef_im * b_im
    bbi_ref[...] = coef_re * b_im + coef_im * b_re


def s5_prep(a_re, a_im, log_dt, b_re, b_im):
    flat = lambda a: a.reshape(1, S5_NSTATE)
    ldt = jnp.broadcast_to(log_dt[:, None], (S5_GROUPS, S5_STATE))
    by_p = lambda b: b.transpose(2, 0, 1).reshape(S5_GROUP, S5_NSTATE)
    row = jax.ShapeDtypeStruct((1, S5_NSTATE), F32)
    mat = jax.ShapeDtypeStruct((S5_GROUP, S5_NSTATE), F32)
    return pl.pallas_call(_s5_prep_kernel, out_shape=[row, row, mat, mat], name="s5_prep")(
        flat(a_re), flat(a_im), flat(ldt), by_p(b_re), by_p(b_im))


def _s5_kernel(u_ref, abr_ref, abi_ref, bre_ref, bim_ref, cre_ref, cim_ref, d_ref, wglu_ref, o_ref,
               hre_ref, him_ref, sre_ref, sim_ref, y_ref, *, tt, nb):
    @pl.when(pl.program_id(0) == 0)
    def _():
        hre_ref[...] = jnp.zeros(hre_ref.shape, F32)
        him_ref[...] = jnp.zeros(him_ref.shape, F32)

    u = u_ref[...].reshape(tt * nb, S5_WIDTH)
    for j in range(S5_WIDTH // S5_LANE_CHUNK):
        uj = u[:, j * S5_LANE_CHUNK:(j + 1) * S5_LANE_CHUNK].astype(BF16)
        ssl = slice(j * S5_STATE_CHUNK, (j + 1) * S5_STATE_CHUNK)
        sre_ref[...] = jnp.dot(uj, bre_ref[j], preferred_element_type=F32)
        sim_ref[...] = jnp.dot(uj, bim_ref[j], preferred_element_type=F32)
        ar = abr_ref[:, ssl]
        ai = abi_ref[:, ssl]

        def step(t, carry):
            hr, hi = carry
            rows = pl.ds(pl.multiple_of(t * nb, nb), nb)
            nr = ar * hr - ai * hi + sre_ref[rows, :]
            ni = ar * hi + ai * hr + sim_ref[rows, :]
            sre_ref[rows, :] = nr
            sim_ref[rows, :] = ni
            return nr, ni

        hr, hi = lax.fori_loop(0, tt, step, (hre_ref[:, ssl], him_ref[:, ssl]), unroll=4)
        hre_ref[:, ssl] = hr
        him_ref[:, ssl] = hi
        y_ref[:, j * S5_LANE_CHUNK:(j + 1) * S5_LANE_CHUNK] = (
            jnp.dot(sre_ref[...].astype(BF16), cre_ref[j], preferred_element_type=F32)
            - jnp.dot(sim_ref[...].astype(BF16), cim_ref[j], preferred_element_type=F32))
    y = jax.nn.gelu(y_ref[...] + d_ref[...] * u)
    out = y * _sigmoid(jnp.dot(y.astype(BF16), wglu_ref[...], preferred_element_type=F32))
    o_ref[...] = out.reshape(tt, nb, S5_WIDTH)


def _group_block_diag(m):
    per = S5_LANE_CHUNK // S5_GROUP
    g, r, c = m.shape
    m = m.reshape(g // per, per, r, c)
    eye = jnp.eye(per, dtype=m.dtype)
    return jnp.einsum("jgrc,gh->jgrhc", m, eye).reshape(g // per, per * r, per * c)


def s5(u_tm, a_re, a_im, b_re, b_im, c_re, c_im, d, log_dt, w_glu, batch, seq):
    abar_re, abar_im, bb_re, bb_im = s5_prep(a_re, a_im, log_dt, b_re, b_im)
    to_groups = lambda bb: bb.reshape(S5_GROUP, S5_GROUPS, S5_STATE).transpose(1, 0, 2)
    bre = _group_block_diag(to_groups(bb_re)).astype(BF16)
    bim = _group_block_diag(to_groups(bb_im)).astype(BF16)
    cre = _group_block_diag(c_re.transpose(0, 2, 1)).astype(BF16)
    cim = _group_block_diag(c_im.transpose(0, 2, 1)).astype(BF16)
    tt = min(S5_TILE, seq)
    u3 = u_tm.reshape(seq, batch, S5_WIDTH)
    blk = pl.BlockSpec((tt, batch, S5_WIDTH), lambda i: (i, 0, 0))
    rows = tt * batch
    out = pl.pallas_call(
        functools.partial(_s5_kernel, tt=tt, nb=batch),
        grid=(seq // tt,),
        in_specs=[blk, _resident((1, S5_NSTATE)), _resident((1, S5_NSTATE)),
                  _resident(bre.shape), _resident(bim.shape), _resident(cre.shape),
                  _resident(cim.shape), _resident((1, S5_WIDTH)), _resident((S5_WIDTH, S5_WIDTH))],
        out_specs=blk,
        out_shape=jax.ShapeDtypeStruct((seq, batch, S5_WIDTH), F32),
        scratch_shapes=[pltpu.VMEM((batch, S5_NSTATE), F32), pltpu.VMEM((batch, S5_NSTATE), F32),
                        pltpu.VMEM((rows, S5_STATE_CHUNK), F32), pltpu.VMEM((rows, S5_STATE_CHUNK), F32),
                        pltpu.VMEM((rows, S5_WIDTH), F32)],
        compiler_params=_params("arbitrary"),
        name="s5",
    )(u3, abar_re, abar_im, bre, bim, cre, cim, d.reshape(1, S5_WIDTH), w_glu.astype(BF16))
    return out.reshape(seq, batch * S5_WIDTH)


def _softplus(x):
    return jnp.maximum(x, 0.0) + jnp.log(1.0 + jnp.exp(-jnp.abs(x)))


def _split2(x):
    hi = x.astype(BF16)
    return hi, (x - hi.astype(F32)).astype(BF16)


def _mm3(a, b):
    ah, al = _split2(a)
    bh, bl = _split2(b)
    return jnp.dot(jnp.concatenate([ah, al, ah], axis=1), jnp.concatenate([bh, bh, bl], axis=0),
                   preferred_element_type=F32)


def _mm_sel(sel, x):
    x1 = x.astype(BF16)
    r1 = x - x1.astype(F32)
    x2 = r1.astype(BF16)
    x3 = (r1 - x2.astype(F32)).astype(BF16)
    s = sel.astype(BF16)
    return jnp.dot(jnp.concatenate([s, s, s], axis=1), jnp.concatenate([x1, x2, x3], axis=0),
                   preferred_element_type=F32)


def _gdn_kernel(ql_ref, kl_ref, vl_ref, gate_ref, bd_ref, cw_ref, alog_ref, dtb_ref, nw_ref, o_ref,
                s_ref, tail_ref, *, tl):
    nc = tl // CHUNK

    @pl.when(pl.program_id(1) == 0)
    def _():
        s_ref[...] = jnp.zeros(s_ref.shape, F32)
        tail_ref[...] = jnp.zeros(tail_ref.shape, F32)

    row_id = _iota2((tl, 1), 0)

    def shift_rows(z, carry_row):
        return jnp.where(row_id == 0, carry_row, pltpu.roll(z, 1, 0))

    def conv_silu(x_ref, idx):
        x = x_ref[...]
        w = cw_ref[idx]
        t = tail_ref[idx]
        tail_ref[idx] = x[tl - SUBLANES:tl, :]
        xm1, xm2, xm3 = (t[SUBLANES - i:SUBLANES - i + 1] for i in (1, 2, 3))
        w0, w1, w2, w3 = (w[j:j + 1] for j in range(CONV_W))
        z = w0 * x
        z = w1 * x + shift_rows(z, w0 * xm1)
        z = w2 * x + shift_rows(z, w1 * xm1 + w0 * xm2)
        z = w3 * x + shift_rows(z, w2 * xm1 + w1 * xm2 + w0 * xm3)
        return _silu(z)

    q = conv_silu(ql_ref, 0)
    k = conv_silu(kl_ref, 1)
    v = conv_silu(vl_ref, 2)

    rt = _iota2((tl, tl), 0)
    ct = _iota2((tl, tl), 1)
    same = (rt // CHUNK) == (ct // CHUNK)
    lower = same & (rt >= ct)
    strict = same & (rt > ct)
    rs = _iota2((CHUNK, tl), 0)
    cs = _iota2((CHUNK, tl), 1)
    eye_side = (rs == cs % CHUNK).astype(F32)
    side_block = cs // CHUNK

    def to_diag(side):
        return jnp.where(same, jnp.concatenate([side] * nc, axis=0), 0.0)

    def from_diag(full):
        out = jnp.zeros((CHUNK, tl), F32)
        for c in range(nc):
            out = out + jnp.where(side_block == c, full[c * CHUNK:(c + 1) * CHUNK, :], 0.0)
        return out

    bd = bd_ref[...]
    beta_all = _sigmoid(bd)
    la_all = -jnp.exp(alog_ref[...]) * _softplus(bd + dtb_ref[...])
    g_all = _mm_sel(lower, la_all)
    g_t = g_all.T
    nw = nw_ref[...]

    heads = range(GDN_HEADS)
    kn, qg, vkb, attn, gcols, n_side, t_side = [], [], [], [], [], [], []
    for h in heads:
        sl = slice(h * GDN_DK, (h + 1) * GDN_DK)
        qh, kh, vh = q[:, sl], k[:, sl], v[:, sl]
        qh = qh * lax.rsqrt(jnp.sum(qh * qh, axis=-1, keepdims=True) + NORM_EPS) * (GDN_DK ** -0.5)
        kh = kh * lax.rsqrt(jnp.sum(kh * kh, axis=-1, keepdims=True) + NORM_EPS)
        beta = beta_all[:, h:h + 1]
        gcol = g_all[:, GDN_HEADS + h:GDN_HEADS + h + 1]
        grow = g_t[GDN_HEADS + h:GDN_HEADS + h + 1, :]
        lmask = jnp.where(lower, jnp.exp(jnp.minimum(gcol - grow, 0.0)), 0.0)
        kb = kh * beta
        kq = _mm_nt(jnp.concatenate([kb, qh], axis=0), kh)
        m = jnp.where(strict, kq[:tl] * lmask, 0.0)
        attn.append(kq[tl:] * lmask)
        eg = jnp.exp(gcol)
        n = -from_diag(m)
        n_side.append(n)
        t_side.append(eye_side + n)
        vkb.append(jnp.concatenate([vh * beta, kb * eg], axis=1))
        kn.append(kh)
        qg.append(qh * eg)
        gcols.append(gcol)

    for h in heads:
        n_side[h] = _mm(n_side[h], to_diag(n_side[h]))
    for _ in range(4):
        for h in heads:
            r = _mm(jnp.concatenate([t_side[h], n_side[h]], axis=0), to_diag(n_side[h]))
            t_side[h] = t_side[h] + r[:CHUNK]
            n_side[h] = r[CHUNK:]
    uw = []
    for h in heads:
        t_inv = t_side[h] + _mm(t_side[h], to_diag(n_side[h]))
        uw.append(_mm(to_diag(t_inv), vkb[h]))

    s = [s_ref[h] for h in heads]
    v_new = [[] for _ in heads]
    o_state = [[] for _ in heads]
    for c in range(nc):
        rows = slice(c * CHUNK, (c + 1) * CHUNK)
        ws = [_mm(jnp.concatenate([uw[h][rows, GDN_DV:], qg[h][rows]], axis=0), s[h]) for h in heads]
        for h in heads:
            vn = uw[h][rows, :GDN_DV] - ws[h][:CHUNK]
            v_new[h].append(vn)
            o_state[h].append(ws[h][CHUNK:])
            gc = gcols[h][rows]
            glast = gc[CHUNK - 1:CHUNK, :]
            s[h] = s[h] * jnp.exp(glast) + _mm_tn(kn[h][rows] * jnp.exp(glast - gc), vn)
    for h in heads:
        s_ref[h] = s[h]
        sl = slice(h * GDN_DK, (h + 1) * GDN_DK)
        o = jnp.concatenate(o_state[h], axis=0) + _mm(attn[h], jnp.concatenate(v_new[h], axis=0))
        o = o * lax.rsqrt(jnp.mean(o * o, axis=-1, keepdims=True) + NORM_EPS) * nw
        o_ref[:, sl] = o * _silu(gate_ref[:, sl])


def gdn(q_lin, k_lin, v_lin, gate, bd, conv_w, a_log, dt_bias, norm_w, batch, seq):
    t = q_lin.shape[0]
    tl = min(MIX_TILE, seq)
    steps = seq // tl
    wide = pl.BlockSpec((tl, GDN_QK), lambda b, j: (b * steps + j, 0))
    narrow = pl.BlockSpec((tl, LANES), lambda b, j: (b * steps + j, 0))
    cw = conv_w.reshape(CONV_W, 3, GDN_QK).transpose(1, 0, 2)
    lane_row = lambda p: jnp.zeros((1, LANES), F32).at[0, GDN_HEADS:2 * GDN_HEADS].set(p)
    return pl.pallas_call(
        functools.partial(_gdn_kernel, tl=tl),
        grid=(batch, steps),
        in_specs=[wide, wide, wide, wide, narrow, _resident((3, CONV_W, GDN_QK)),
                  _resident((1, LANES)), _resident((1, LANES)), _resident((1, GDN_DV))],
        out_specs=wide,
        out_shape=jax.ShapeDtypeStruct((t, GDN_V), F32),
        scratch_shapes=[pltpu.VMEM((GDN_HEADS, GDN_DK, GDN_DV), F32),
                        pltpu.VMEM((3, SUBLANES, GDN_QK), F32)],
        compiler_params=_params("parallel", "arbitrary"),
        name="gdn",
    )(q_lin, k_lin, v_lin, gate, bd, cw, lane_row(a_log), lane_row(dt_bias),
      norm_w.reshape(1, GDN_DV))


def kernel(x, ffn1_norm, ffn1_w_gate, ffn1_w_up, ffn1_w_down, mix_norm, ffn2_norm, ffn2_w_gate, ffn2_w_up, ffn2_w_down, ev_w_in, hg_lb_logits, hg_norm_w, s5_a_re, s5_a_im, s5_b_re, s5_b_im, s5_c_re, s5_c_im, s5_d, s5_log_dt, s5_w_glu, ev_w_out, od_w_in, gdn_conv_w, gdn_a_log, gdn_dt_bias, gdn_norm_w, od_w_out, final_norm):
    batch, seq, _ = x.shape
    xf = x.reshape(batch * seq, D_MODEL)
    for layer in range(DEPTH):
        xf = ffn(xf, ffn1_norm[layer], ffn1_w_gate[layer], ffn1_w_up[layer], ffn1_w_down[layer], seq)
        j = layer // 2
        if layer % 2 == 0:
            q_lin, f_lin, i_val, g_lin, u_tm = norm_proj(
                xf, mix_norm[layer], ev_w_in[j], (HG_QK, HG_QK, HG_WIDTH, HG_WIDTH, S5_WIDTH), seq,
                time_major_last=True)
            y_a = hgrn2(q_lin, f_lin, i_val, g_lin, hg_lb_logits, hg_norm_w[j], j, batch, seq)
            y_b = s5(u_tm, s5_a_re[j], s5_a_im[j], s5_b_re[j], s5_b_im[j], s5_c_re[j], s5_c_im[j],
                     s5_d[j], s5_log_dt[j], s5_w_glu[j], batch, seq)
            w_out = ev_w_out[j]
            parts = [(y_a, w_out[:HG_WIDTH], False), (y_b, w_out[HG_WIDTH:], True)]
        else:
            n_main = 2 * GDN_QK + 2 * GDN_V
            w_in = od_w_in[j]
            w_in = jnp.pad(w_in, ((0, 0), (0, LANES - (w_in.shape[1] - n_main))))
            q_lin, k_lin, v_lin, gate, bd = norm_proj(
                xf, mix_norm[layer], w_in, (GDN_QK, GDN_QK, GDN_V, GDN_V, LANES), seq)
            y = gdn(q_lin, k_lin, v_lin, gate, bd, gdn_conv_w[j], gdn_a_log[j], gdn_dt_bias[j],
                    gdn_norm_w[j], batch, seq)
            parts = [(y, od_w_out[j], False)]
        last = layer == DEPTH - 1
        xf = ffn(xf, ffn2_norm[layer], ffn2_w_gate[layer], ffn2_w_up[layer], ffn2_w_down[layer], seq,
                 parts=parts, final_w=final_norm if last else None)
    return xf.reshape(batch, seq, D_MODEL)
```

```python
import functools

import jax
import jax.numpy as jnp
from jax import lax
from jax.experimental import pallas as pl
from jax.experimental.pallas import tpu as pltpu

F32 = jnp.float32
BF16 = jnp.bfloat16
HIGHEST = lax.Precision.HIGHEST

D_MODEL = 1024
DEPTH = 4
D_FF = 2816
NORM_EPS = 1e-6
F_MIN = 1e-6
CHUNK = 64
HG_HEADS = 4
HG_DK = 128
HG_DV = 128
HG_QK = HG_HEADS * HG_DK
HG_WIDTH = HG_HEADS * HG_DV
S5_WIDTH = D_MODEL - HG_WIDTH
S5_GROUP = 16
S5_GROUPS = S5_WIDTH // S5_GROUP
S5_STATE = 64
S5_NSTATE = S5_GROUPS * S5_STATE
GDN_HEADS = 8
GDN_DK = 128
GDN_DV = 128
GDN_QK = GDN_HEADS * GDN_DK
GDN_V = GDN_HEADS * GDN_DV
CONV_W = 4

LANES = 128
SUBLANES = 8
VMEM_LIMIT_BYTES = 56 * 1024 * 1024

TOKEN_TILE = 512
FF_BLOCK = 256
MIX_TILE = 256
S5_TILE = 64
S5_LANE_CHUNK = 128
S5_STATE_CHUNK = S5_LANE_CHUNK // S5_GROUP * S5_STATE


def _mm(a, b):
    return jnp.dot(a.astype(BF16), b.astype(BF16), preferred_element_type=F32)


def _mm_nt(a, b):
    return lax.dot_general(a.astype(BF16), b.astype(BF16), (((1,), (1,)), ((), ())),
                           preferred_element_type=F32)


def _mm_tn(a, b):
    return lax.dot_general(a.astype(BF16), b.astype(BF16), (((0,), (0,)), ((), ())),
                           preferred_element_type=F32)


def _mm_exact(a, b):
    return jnp.dot(a, b, precision=HIGHEST, preferred_element_type=F32)


def _sigmoid(x):
    return 1.0 / (1.0 + jnp.exp(-x))


def _silu(x):
    return x * _sigmoid(x)


def _rmsnorm(x, w):
    return x * lax.rsqrt(jnp.mean(x * x, axis=-1, keepdims=True) + NORM_EPS) * w


def _resident(shape):
    nd = len(shape)
    return pl.BlockSpec(shape, lambda *_: (0,) * nd, pipeline_mode=pl.Buffered(1))


def _params(*sem):
    return pltpu.CompilerParams(dimension_semantics=sem, vmem_limit_bytes=VMEM_LIMIT_BYTES)


def _ffn_kernel(*refs, n_parts, final):
    x_ref, refs = refs[0], refs[1:]
    part_refs, refs = refs[:2 * n_parts], refs[2 * n_parts:]
    if final:
        nw_ref, wg_ref, wu_ref, wd_ref, fw_ref, o_ref = refs
    else:
        nw_ref, wg_ref, wu_ref, wd_ref, o_ref = refs
    x = x_ref[...]
    for y_ref, w_ref in zip(part_refs[0::2], part_refs[1::2]):
        x = x + jnp.dot(y_ref[...].astype(BF16), w_ref[...], preferred_element_type=F32)
    h = _rmsnorm(x, nw_ref[...]).astype(BF16)
    acc = jnp.zeros(x.shape, F32)
    for j in range(D_FF // FF_BLOCK):
        sl = slice(j * FF_BLOCK, (j + 1) * FF_BLOCK)
        g = jnp.dot(h, wg_ref[:, sl], preferred_element_type=F32)
        u = jnp.dot(h, wu_ref[:, sl], preferred_element_type=F32)
        a = (_silu(g) * u).astype(BF16)
        acc = acc + jnp.dot(a, wd_ref[sl, :], preferred_element_type=F32)
    y = x + 0.5 * acc
    if final:
        y = _rmsnorm(y, fw_ref[...])
    o_ref[...] = y


def ffn(x, norm_w, w_gate, w_up, w_down, seq, parts=(), final_w=None):
    t = x.shape[0]
    tm = min(TOKEN_TILE, seq)
    blocks_per_seq = seq // tm
    final = final_w is not None
    row = pl.BlockSpec((tm, D_MODEL), lambda i: (i, 0))
    in_specs, args = [row], [x]
    for y, w, time_major in parts:
        n = w.shape[0]
        if time_major:
            in_specs.append(pl.BlockSpec(
                (tm, n), lambda i: (i % blocks_per_seq, i // blocks_per_seq)))
        else:
            in_specs.append(pl.BlockSpec((tm, n), lambda i: (i, 0)))
        in_specs.append(_resident((n, D_MODEL)))
        args += [y, w.astype(BF16)]
    in_specs += [_resident((1, D_MODEL)), _resident((D_MODEL, D_FF)),
                 _resident((D_MODEL, D_FF)), _resident((D_FF, D_MODEL))]
    args += [norm_w.reshape(1, D_MODEL), w_gate.astype(BF16), w_up.astype(BF16),
             w_down.astype(BF16)]
    if final:
        in_specs.append(_resident((1, D_MODEL)))
        args.append(final_w.reshape(1, D_MODEL))
    return pl.pallas_call(
        functools.partial(_ffn_kernel, n_parts=len(parts), final=final),
        grid=(t // tm,),
        in_specs=in_specs,
        out_specs=row,
        out_shape=jax.ShapeDtypeStruct((t, D_MODEL), F32),
        compiler_params=_params("parallel"),
        name=("ffn_mix" if parts else "ffn") + ("_final" if final else ""),
    )(*args)


def _proj(hn, w_ref, start, width):
    return jnp.dot(hn, w_ref[:, start:start + width], preferred_element_type=F32)


HEAD_PAIR = 2 * LANES


def _iota2(shape, axis):
    return lax.broadcasted_iota(jnp.int32, shape, axis)


def _hgrn2_kernel(x_ref, mnw_ref, w_ref, lbl_ref, nw_ref, o_ref, u_ref, st_ref, *, layer, tl):
    nc = tl // CHUNK
    hn = _rmsnorm(x_ref[...], mnw_ref[...]).astype(BF16)
    u_ref[...] = _proj(hn, w_ref, 2 * HG_QK + 2 * HG_WIDTH, S5_WIDTH)

    @pl.when(pl.program_id(1) == 0)
    def _():
        st_ref[...] = jnp.zeros(st_ref.shape, F32)

    logits = lbl_ref[...]
    e = jnp.exp(logits - jnp.max(logits, axis=0, keepdims=True))
    p = e / jnp.sum(e, axis=0, keepdims=True)
    lb = jnp.sum(p[0:layer + 1], axis=0, keepdims=True) - p[0:1]
    nw = nw_ref[...]

    rt = _iota2((tl, tl), 0)
    ct = _iota2((tl, tl), 1)
    chunk_lower = ((rt // CHUNK) == (ct // CHUNK)) & (rt >= ct)
    levels = (32, 16, 8, 4)
    level_mask = {m: ((rt // (2 * m)) == (ct // (2 * m))) & ((rt // m) % 2 == 1) & ((ct // m) % 2 == 0)
                  for m in levels}
    base = 4
    base_mask = ((rt // base) == (ct // base)) & (rt >= ct)
    pos = _iota2((tl, 1), 0) % base

    pairs = []
    for pr in range(HG_QK // HEAD_PAIR):
        c0 = pr * HEAD_PAIR
        lbp = lb[:, c0:c0 + HEAD_PAIR]
        q = _silu(_proj(hn, w_ref, c0, HEAD_PAIR))
        f = lbp + (1.0 - lbp) * _sigmoid(_proj(hn, w_ref, HG_QK + c0, HEAD_PAIR))
        lf = jnp.log(jnp.maximum(f, F_MIN))
        v = _proj(hn, w_ref, 2 * HG_QK + c0, HEAD_PAIR)
        gate = _silu(_proj(hn, w_ref, 2 * HG_QK + HG_WIDTH + c0, HEAD_PAIR))
        b = _mm_sel(chunk_lower, lf)
        c4 = lf
        for j in range(1, base):
            c4 = c4 + jnp.where(pos >= j, pltpu.roll(lf, j, 0), 0.0)
        pairs.append((b, q, 1.0 - f, v, c4, gate))

    outs = []
    for h in range(HG_HEADS):
        sl = slice(h * HG_DK % HEAD_PAIR, h * HG_DK % HEAD_PAIR + HG_DK)
        bh, qh, kh, vh, ch, gh = (a[:, sl] for a in pairs[h * HG_DK // HEAD_PAIR])
        attn = jnp.where(base_mask, _mm_nt(qh * jnp.exp(ch), kh * jnp.exp(-ch)), 0.0)
        for m in levels:
            if m == base:
                b3 = bh.reshape(tl // (2 * m), 2 * m, HG_DK)
                ref_q = ref_k = b3[:, m - 1:m, :]
            else:
                b3 = bh.reshape(tl // m, m, HG_DK)
                ref_k = b3[:, m - 1:m, :]
                ref_q = jnp.concatenate([jnp.zeros((1, 1, HG_DK), F32), ref_k[:-1]], axis=0)
            qt = qh * jnp.exp(jnp.minimum(b3 - ref_q, 0.0)).reshape(tl, HG_DK)
            kt = kh * jnp.exp(jnp.minimum(ref_k - b3, 0.0)).reshape(tl, HG_DK)
            attn = jnp.where(level_mask[m], _mm_nt(qt, kt), attn)
        o = _mm(attn, vh)
        bc = bh.reshape(nc, CHUNK, HG_DK)
        blast = bc[:, CHUNK - 1:CHUNK, :]
        ks = kh * jnp.exp(blast - bc).reshape(tl, HG_DK)
        qe = qh * jnp.exp(bh)
        st = st_ref[h]
        o_state = []
        for c in range(nc):
            rows = slice(c * CHUNK, (c + 1) * CHUNK)
            o_state.append(_mm_nt(qe[rows], st))
            st = st * jnp.exp(blast[c]) + _mm_tn(vh[rows], ks[rows])
        st_ref[h] = st
        o = o + jnp.concatenate(o_state, axis=0)
        o = o * lax.rsqrt(jnp.mean(o * o, axis=-1, keepdims=True) + NORM_EPS) * nw
        outs.append(o * gh)
    o_ref[...] = jnp.concatenate(outs, axis=-1)


def hgrn2(x, mix_norm_w, w_in, lb_logits, norm_w, layer, batch, seq):
    t = x.shape[0]
    tl = min(MIX_TILE, seq)
    steps = seq // tl
    n_layers = lb_logits.shape[0]
    return pl.pallas_call(
        functools.partial(_hgrn2_kernel, layer=layer, tl=tl),
        grid=(batch, steps),
        in_specs=[pl.BlockSpec((tl, D_MODEL), lambda b, j: (b * steps + j, 0)), _resident((1, D_MODEL)),
                  _resident(w_in.shape), _resident((n_layers, HG_QK)), _resident((1, HG_DV))],
        out_specs=[pl.BlockSpec((tl, HG_WIDTH), lambda b, j: (b * steps + j, 0)),
                   pl.BlockSpec((tl, S5_WIDTH), lambda b, j: (j, b))],
        out_shape=[jax.ShapeDtypeStruct((t, HG_WIDTH), F32),
                   jax.ShapeDtypeStruct((seq, batch * S5_WIDTH), F32)],
        scratch_shapes=[pltpu.VMEM((HG_HEADS, HG_DV, HG_DK), F32)],
        compiler_params=_params("parallel", "arbitrary"),
        name="hgrn2",
    )(x, mix_norm_w.reshape(1, D_MODEL), w_in.astype(BF16), lb_logits, norm_w.reshape(1, HG_DV))


def _s5_prep_kernel(are_ref, aim_ref, ldt_ref, bre_ref, bim_ref, abr_ref, abi_ref, bbr_ref, bbi_ref):
    a_re, a_im = are_ref[...], aim_ref[...]
    dt = jnp.exp(ldt_ref[...])
    mag = jnp.exp(dt * a_re)
    ang = dt * a_im
    abar_re = mag * jnp.cos(ang)
    abar_im = mag * jnp.sin(ang)
    den = a_re * a_re + a_im * a_im
    zr = abar_re - 1.0
    zi = abar_im
    coef_re = (zr * a_re + zi * a_im) / den
    coef_im = (zi * a_re - zr * a_im) / den
    b_re, b_im = bre_ref[...], bim_ref[...]
    abr_ref[...] = abar_re
    abi_ref[...] = abar_im
    bbr_ref[...] = coef_re * b_re - coef_im * b_im
    bbi_ref[...] = coef_re * b_im + coef_im * b_re


def s5_prep(a_re, a_im, log_dt, b_re, b_im):
    flat = lambda a: a.reshape(1, S5_NSTATE)
    ldt = jnp.broadcast_to(log_dt[:, None], (S5_GROUPS, S5_STATE))
    by_p = lambda b: b.transpose(2, 0, 1).reshape(S5_GROUP, S5_NSTATE)
    row = jax.ShapeDtypeStruct((1, S5_NSTATE), F32)
    mat = jax.ShapeDtypeStruct((S5_GROUP, S5_NSTATE), F32)
    return pl.pallas_call(_s5_prep_kernel, out_shape=[row, row, mat, mat], name="s5_prep")(
        flat(a_re), flat(a_im), flat(ldt), by_p(b_re), by_p(b_im))


def _s5_kernel(u_ref, abr_ref, abi_ref, bre_ref, bim_ref, cre_ref, cim_ref, d_ref, wglu_ref, o_ref,
               hre_ref, him_ref, sre_ref, sim_ref, y_ref, *, tt, nb):
    @pl.when(pl.program_id(0) == 0)
    def _():
        hre_ref[...] = jnp.zeros(hre_ref.shape, F32)
        him_ref[...] = jnp.zeros(him_ref.shape, F32)

    u = u_ref[...].reshape(tt * nb, S5_WIDTH)
    for j in range(S5_WIDTH // S5_LANE_CHUNK):
        uj = u[:, j * S5_LANE_CHUNK:(j + 1) * S5_LANE_CHUNK].astype(BF16)
        ssl = slice(j * S5_STATE_CHUNK, (j + 1) * S5_STATE_CHUNK)
        sre_ref[...] = jnp.dot(uj, bre_ref[j], preferred_element_type=F32)
        sim_ref[...] = jnp.dot(uj, bim_ref[j], preferred_element_type=F32)
        ar = abr_ref[:, ssl]
        ai = abi_ref[:, ssl]

        def step(t, carry):
            hr, hi = carry
            rows = pl.ds(pl.multiple_of(t * nb, nb), nb)
            nr = ar * hr - ai * hi + sre_ref[rows, :]
            ni = ar * hi + ai * hr + sim_ref[rows, :]
            sre_ref[rows, :] = nr
            sim_ref[rows, :] = ni
            return nr, ni

        hr, hi = lax.fori_loop(0, tt, step, (hre_ref[:, ssl], him_ref[:, ssl]), unroll=4)
        hre_ref[:, ssl] = hr
        him_ref[:, ssl] = hi
        y_ref[:, j * S5_LANE_CHUNK:(j + 1) * S5_LANE_CHUNK] = (
            jnp.dot(sre_ref[...].astype(BF16), cre_ref[j], preferred_element_type=F32)
            - jnp.dot(sim_ref[...].astype(BF16), cim_ref[j], preferred_element_type=F32))
    y = jax.nn.gelu(y_ref[...] + d_ref[...] * u)
    out = y * _sigmoid(jnp.dot(y.astype(BF16), wglu_ref[...], preferred_element_type=F32))
    o_ref[...] = out.reshape(tt, nb, S5_WIDTH)


def _group_block_diag(m):
    per = S5_LANE_CHUNK // S5_GROUP
    g, r, c = m.shape
    m = m.reshape(g // per, per, r, c)
    eye = jnp.eye(per, dtype=m.dtype)
    return jnp.einsum("jgrc,gh->jgrhc", m, eye).reshape(g // per, per * r, per * c)


def s5(u_tm, a_re, a_im, b_re, b_im, c_re, c_im, d, log_dt, w_glu, batch, seq):
    abar_re, abar_im, bb_re, bb_im = s5_prep(a_re, a_im, log_dt, b_re, b_im)
    to_groups = lambda bb: bb.reshape(S5_GROUP, S5_GROUPS, S5_STATE).transpose(1, 0, 2)
    bre = _group_block_diag(to_groups(bb_re)).astype(BF16)
    bim = _group_block_diag(to_groups(bb_im)).astype(BF16)
    cre = _group_block_diag(c_re.transpose(0, 2, 1)).astype(BF16)
    cim = _group_block_diag(c_im.transpose(0, 2, 1)).astype(BF16)
    tt = min(S5_TILE, seq)
    u3 = u_tm.reshape(seq, batch, S5_WIDTH)
    blk = pl.BlockSpec((tt, batch, S5_WIDTH), lambda i: (i, 0, 0))
    rows = tt * batch
    out = pl.pallas_call(
        functools.partial(_s5_kernel, tt=tt, nb=batch),
        grid=(seq // tt,),
        in_specs=[blk, _resident((1, S5_NSTATE)), _resident((1, S5_NSTATE)),
                  _resident(bre.shape), _resident(bim.shape), _resident(cre.shape),
                  _resident(cim.shape), _resident((1, S5_WIDTH)), _resident((S5_WIDTH, S5_WIDTH))],
        out_specs=blk,
        out_shape=jax.ShapeDtypeStruct((seq, batch, S5_WIDTH), F32),
        scratch_shapes=[pltpu.VMEM((batch, S5_NSTATE), F32), pltpu.VMEM((batch, S5_NSTATE), F32),
                        pltpu.VMEM((rows, S5_STATE_CHUNK), F32), pltpu.VMEM((rows, S5_STATE_CHUNK), F32),
                        pltpu.VMEM((rows, S5_WIDTH), F32)],
        compiler_params=_params("arbitrary"),
        name="s5",
    )(u3, abar_re, abar_im, bre, bim, cre, cim, d.reshape(1, S5_WIDTH), w_glu.astype(BF16))
    return out.reshape(seq, batch * S5_WIDTH)


def _softplus(x):
    return jnp.maximum(x, 0.0) + jnp.log(1.0 + jnp.exp(-jnp.abs(x)))


def _split2(x):
    hi = x.astype(BF16)
    return hi, (x - hi.astype(F32)).astype(BF16)


def _mm3(a, b):
    ah, al = _split2(a)
    bh, bl = _split2(b)
    return jnp.dot(jnp.concatenate([ah, al, ah], axis=1), jnp.concatenate([bh, bh, bl], axis=0),
                   preferred_element_type=F32)


def _mm_sel(sel, x):
    x1 = x.astype(BF16)
    r1 = x - x1.astype(F32)
    x2 = r1.astype(BF16)
    x3 = (r1 - x2.astype(F32)).astype(BF16)
    s = sel.astype(BF16)
    return jnp.dot(jnp.concatenate([s, s, s], axis=1), jnp.concatenate([x1, x2, x3], axis=0),
                   preferred_element_type=F32)


def _gdn_kernel(x_ref, mnw_ref, w_ref, cw_ref, alog_ref, dtb_ref, nw_ref, o_ref, s_ref, tail_ref, *, tl):
    nc = tl // CHUNK

    @pl.when(pl.program_id(1) == 0)
    def _():
        s_ref[...] = jnp.zeros(s_ref.shape, F32)
        tail_ref[...] = jnp.zeros(tail_ref.shape, F32)

    hn = _rmsnorm(x_ref[...], mnw_ref[...]).astype(BF16)
    row_id = _iota2((tl, 1), 0)

    def shift_rows(z, carry_row):
        return jnp.where(row_id == 0, carry_row, pltpu.roll(z, 1, 0))

    def conv_silu(idx, c0):
        x = _proj(hn, w_ref, idx * GDN_QK + c0, HEAD_PAIR)
        w = cw_ref[idx, :, c0:c0 + HEAD_PAIR]
        t = tail_ref[idx, :, c0:c0 + HEAD_PAIR]
        tail_ref[idx, :, c0:c0 + HEAD_PAIR] = x[tl - SUBLANES:tl, :]
        xm1, xm2, xm3 = (t[SUBLANES - i:SUBLANES - i + 1] for i in (1, 2, 3))
        w0, w1, w2, w3 = (w[j:j + 1] for j in range(CONV_W))
        z = w0 * x
        z = w1 * x + shift_rows(z, w0 * xm1)
        z = w2 * x + shift_rows(z, w1 * xm1 + w0 * xm2)
        z = w3 * x + shift_rows(z, w2 * xm1 + w1 * xm2 + w0 * xm3)
        return _silu(z)

    pairs = []
    for pr in range(GDN_QK // HEAD_PAIR):
        c0 = pr * HEAD_PAIR
        pairs.append((conv_silu(0, c0), conv_silu(1, c0), conv_silu(2, c0),
                      _silu(_proj(hn, w_ref, 3 * GDN_QK + c0, HEAD_PAIR))))

    rt = _iota2((tl, tl), 0)
    ct = _iota2((tl, tl), 1)
    same = (rt // CHUNK) == (ct // CHUNK)
    lower = same & (rt >= ct)
    strict = same & (rt > ct)
    rs = _iota2((CHUNK, tl), 0)
    cs = _iota2((CHUNK, tl), 1)
    eye_side = (rs == cs % CHUNK).astype(F32)
    side_block = cs // CHUNK

    def to_diag(side):
        return jnp.where(same, jnp.concatenate([side] * nc, axis=0), 0.0)

    def from_diag(full):
        out = jnp.zeros((CHUNK, tl), F32)
        for c in range(nc):
            out = out + jnp.where(side_block == c, full[c * CHUNK:(c + 1) * CHUNK, :], 0.0)
        return out

    bd = _proj(hn, w_ref, 4 * GDN_QK, LANES)
    beta_all = _sigmoid(bd)
    la_all = -jnp.exp(alog_ref[...]) * _softplus(bd + dtb_ref[...])
    g_all = _mm_sel(lower, la_all)
    g_t = g_all.T
    nw = nw_ref[...]

    heads = range(GDN_HEADS)
    kn, qg, vkb, attn, gcols, n_side, t_side, gates = [], [], [], [], [], [], [], []
    for h in heads:
        sl = slice(h * GDN_DK % HEAD_PAIR, h * GDN_DK % HEAD_PAIR + GDN_DK)
        qh, kh, vh, gh = (a[:, sl] for a in pairs[h * GDN_DK // HEAD_PAIR])
        gates.append(gh)
        qh = qh * lax.rsqrt(jnp.sum(qh * qh, axis=-1, keepdims=True) + NORM_EPS) * (GDN_DK ** -0.5)
        kh = kh * lax.rsqrt(jnp.sum(kh * kh, axis=-1, keepdims=True) + NORM_EPS)
        beta = beta_all[:, h:h + 1]
        gcol = g_all[:, GDN_HEADS + h:GDN_HEADS + h + 1]
        grow = g_t[GDN_HEADS + h:GDN_HEADS + h + 1, :]
        lmask = jnp.where(lower, jnp.exp(jnp.minimum(gcol - grow, 0.0)), 0.0)
        kb = kh * beta
        kq = _mm_nt(jnp.concatenate([kb, qh], axis=0), kh)
        m = jnp.where(strict, kq[:tl] * lmask, 0.0)
        attn.append(kq[tl:] * lmask)
        eg = jnp.exp(gcol)
        n = -from_diag(m)
        n_side.append(n)
        t_side.append(eye_side + n)
        vkb.append(jnp.concatenate([vh * beta, kb * eg], axis=1))
        kn.append(kh)
        qg.append(qh * eg)
        gcols.append(gcol)

    for h in heads:
        n_side[h] = _mm(n_side[h], to_diag(n_side[h]))
    for _ in range(4):
        for h in heads:
            r = _mm(jnp.concatenate([t_side[h], n_side[h]], axis=0), to_diag(n_side[h]))
            t_side[h] = t_side[h] + r[:CHUNK]
            n_side[h] = r[CHUNK:]
    uw = []
    for h in heads:
        t_inv = t_side[h] + _mm(t_side[h], to_diag(n_side[h]))
        uw.append(_mm(to_diag(t_inv), vkb[h]))

    s = [s_ref[h] for h in heads]
    v_new = [[] for _ in heads]
    o_state = [[] for _ in heads]
    for c in range(nc):
        rows = slice(c * CHUNK, (c + 1) * CHUNK)
        ws = [_mm(jnp.concatenate([uw[h][rows, GDN_DV:], qg[h][rows]], axis=0), s[h]) for h in heads]
        for h in heads:
            vn = uw[h][rows, :GDN_DV] - ws[h][:CHUNK]
            v_new[h].append(vn)
            o_state[h].append(ws[h][CHUNK:])
            gc = gcols[h][rows]
            glast = gc[CHUNK - 1:CHUNK, :]
            s[h] = s[h] * jnp.exp(glast) + _mm_tn(kn[h][rows] * jnp.exp(glast - gc), vn)
    for h in heads:
        s_ref[h] = s[h]
        sl = slice(h * GDN_DK, (h + 1) * GDN_DK)
        o = jnp.concatenate(o_state[h], axis=0) + _mm(attn[h], jnp.concatenate(v_new[h], axis=0))
        o = o * lax.rsqrt(jnp.mean(o * o, axis=-1, keepdims=True) + NORM_EPS) * nw
        o_ref[:, sl] = o * gates[h]


def gdn(x, mix_norm_w, w_in, conv_w, a_log, dt_bias, norm_w, batch, seq):
    t = x.shape[0]
    tl = min(MIX_TILE, seq)
    steps = seq // tl
    tile = lambda n: pl.BlockSpec((tl, n), lambda b, j: (b * steps + j, 0))
    n_main = 2 * GDN_QK + 2 * GDN_V
    w_in = jnp.pad(w_in, ((0, 0), (0, LANES - (w_in.shape[1] - n_main))))
    cw = conv_w.reshape(CONV_W, 3, GDN_QK).transpose(1, 0, 2)
    lane_row = lambda p: jnp.zeros((1, LANES), F32).at[0, GDN_HEADS:2 * GDN_HEADS].set(p)
    return pl.pallas_call(
        functools.partial(_gdn_kernel, tl=tl),
        grid=(batch, steps),
        in_specs=[tile(D_MODEL), _resident((1, D_MODEL)), _resident(w_in.shape),
                  _resident((3, CONV_W, GDN_QK)), _resident((1, LANES)), _resident((1, LANES)),
                  _resident((1, GDN_DV))],
        out_specs=tile(GDN_V),
        out_shape=jax.ShapeDtypeStruct((t, GDN_V), F32),
        scratch_shapes=[pltpu.VMEM((GDN_HEADS, GDN_DK, GDN_DV), F32),
                        pltpu.VMEM((3, SUBLANES, GDN_QK), F32)],
        compiler_params=_params("parallel", "arbitrary"),
        name="gdn",
    )(x, mix_norm_w.reshape(1, D_MODEL), w_in.astype(BF16), cw, lane_row(a_log), lane_row(dt_bias),
      norm_w.reshape(1, GDN_DV))


def kernel(x, ffn1_norm, ffn1_w_gate, ffn1_w_up, ffn1_w_down, mix_norm, ffn2_norm, ffn2_w_gate, ffn2_w_up, ffn2_w_down, ev_w_in, hg_lb_logits, hg_norm_w, s5_a_re, s5_a_im, s5_b_re, s5_b_im, s5_c_re, s5_c_im, s5_d, s5_log_dt, s5_w_glu, ev_w_out, od_w_in, gdn_conv_w, gdn_a_log, gdn_dt_bias, gdn_norm_w, od_w_out, final_norm):
    batch, seq, _ = x.shape
    xf = x.reshape(batch * seq, D_MODEL)
    for layer in range(DEPTH):
        xf = ffn(xf, ffn1_norm[layer], ffn1_w_gate[layer], ffn1_w_up[layer], ffn1_w_down[layer], seq)
        j = layer // 2
        if layer % 2 == 0:
            y_a, u_tm = hgrn2(xf, mix_norm[layer], ev_w_in[j], hg_lb_logits, hg_norm_w[j], j, batch, seq)
            y_b = s5(u_tm, s5_a_re[j], s5_a_im[j], s5_b_re[j], s5_b_im[j], s5_c_re[j], s5_c_im[j],
                     s5_d[j], s5_log_dt[j], s5_w_glu[j], batch, seq)
            w_out = ev_w_out[j]
            parts = [(y_a, w_out[:HG_WIDTH], False), (y_b, w_out[HG_WIDTH:], True)]
        else:
            y = gdn(xf, mix_norm[layer], od_w_in[j], gdn_conv_w[j], gdn_a_log[j], gdn_dt_bias[j],
                    gdn_norm_w[j], batch, seq)
            parts = [(y, od_w_out[j], False)]
        last = layer == DEPTH - 1
        xf = ffn(xf, ffn2_norm[layer], ffn2_w_gate[layer], ffn2_w_up[layer], ffn2_w_down[layer], seq,
                 parts=parts, final_w=final_norm if last else None)
    return xf.reshape(batch, seq, D_MODEL)
```

```python
import functools

import jax
import jax.numpy as jnp
from jax import lax
from jax.experimental import pallas as pl
from jax.experimental.pallas import tpu as pltpu

F32 = jnp.float32
BF16 = jnp.bfloat16

D_MODEL = 1024
DEPTH = 4
D_FF = 2816
NORM_EPS = 1e-6
F_MIN = 1e-6
CHUNK = 64
HG_HEADS = 4
HG_DK = 128
HG_DV = 128
HG_QK = HG_HEADS * HG_DK
HG_WIDTH = HG_HEADS * HG_DV
S5_WIDTH = D_MODEL - HG_WIDTH
S5_GROUP = 16
S5_GROUPS = S5_WIDTH // S5_GROUP
S5_STATE = 64
S5_NSTATE = S5_GROUPS * S5_STATE
GDN_HEADS = 8
GDN_DK = 128
GDN_DV = 128
GDN_QK = GDN_HEADS * GDN_DK
GDN_V = GDN_HEADS * GDN_DV
CONV_W = 4

LANES = 128
SUBLANES = 8
VMEM_LIMIT_BYTES = 56 * 1024 * 1024

TOKEN_TILE = 512
FF_BLOCK = 256
MIX_TILE = 256
S5_TILE = 64
S5_LANE_CHUNK = 128
S5_STATE_CHUNK = S5_LANE_CHUNK // S5_GROUP * S5_STATE


def _mm(a, b):
    return jnp.dot(a.astype(BF16), b.astype(BF16), preferred_element_type=F32)


def _mm_nt(a, b):
    return lax.dot_general(a.astype(BF16), b.astype(BF16), (((1,), (1,)), ((), ())),
                           preferred_element_type=F32)


def _mm_tn(a, b):
    return lax.dot_general(a.astype(BF16), b.astype(BF16), (((0,), (0,)), ((), ())),
                           preferred_element_type=F32)


def _sigmoid(x):
    return 1.0 / (1.0 + jnp.exp(-x))


def _silu(x):
    return x * _sigmoid(x)


def _rmsnorm(x, w):
    return x * lax.rsqrt(jnp.mean(x * x, axis=-1, keepdims=True) + NORM_EPS) * w


def _resident(shape):
    nd = len(shape)
    return pl.BlockSpec(shape, lambda *_: (0,) * nd, pipeline_mode=pl.Buffered(1))


def _params(*sem):
    return pltpu.CompilerParams(dimension_semantics=sem, vmem_limit_bytes=VMEM_LIMIT_BYTES)


def _ffn_kernel(*refs, n_parts, final):
    x_ref, refs = refs[0], refs[1:]
    part_refs, refs = refs[:2 * n_parts], refs[2 * n_parts:]
    if final:
        nw_ref, wg_ref, wu_ref, wd_ref, fw_ref, o_ref = refs
    else:
        nw_ref, wg_ref, wu_ref, wd_ref, o_ref = refs
    x = x_ref[...]
    for y_ref, w_ref in zip(part_refs[0::2], part_refs[1::2]):
        x = x + jnp.dot(y_ref[...].astype(BF16), w_ref[...], preferred_element_type=F32)
    h = _rmsnorm(x, nw_ref[...]).astype(BF16)
    acc = jnp.zeros(x.shape, F32)
    for j in range(D_FF // FF_BLOCK):
        sl = slice(j * FF_BLOCK, (j + 1) * FF_BLOCK)
        g = jnp.dot(h, wg_ref[:, sl], preferred_element_type=F32)
        u = jnp.dot(h, wu_ref[:, sl], preferred_element_type=F32)
        a = (_silu(g) * u).astype(BF16)
        acc = acc + jnp.dot(a, wd_ref[sl, :], preferred_element_type=F32)
    y = x + 0.5 * acc
    if final:
        y = _rmsnorm(y, fw_ref[...])
    o_ref[...] = y


def _resident_layer(shape, layer):
    nd = len(shape) - 1
    return pl.BlockSpec((None,) + tuple(shape[1:]), lambda *_: (layer,) + (0,) * nd,
                        pipeline_mode=pl.Buffered(1))


def ffn(x, norm_w, w_gate, w_up, w_down, layer, seq, parts=(), final_w=None):
    t = x.shape[0]
    tm = min(TOKEN_TILE, seq)
    blocks_per_seq = seq // tm
    final = final_w is not None
    row = pl.BlockSpec((tm, D_MODEL), lambda i: (i, 0))
    in_specs, args = [row], [x]
    for y, w, time_major in parts:
        n = w.shape[0]
        if time_major:
            in_specs.append(pl.BlockSpec(
                (tm, n), lambda i: (i % blocks_per_seq, i // blocks_per_seq)))
        else:
            in_specs.append(pl.BlockSpec((tm, n), lambda i: (i, 0)))
        in_specs.append(_resident((n, D_MODEL)))
        args += [y, w.astype(BF16)]
    in_specs += [_resident((1, D_MODEL))] + [_resident_layer(w.shape, layer) for w in (w_gate, w_up, w_down)]
    args += [norm_w.reshape(1, D_MODEL), w_gate, w_up, w_down]
    if final:
        in_specs.append(_resident((1, D_MODEL)))
        args.append(final_w.reshape(1, D_MODEL))
    return pl.pallas_call(
        functools.partial(_ffn_kernel, n_parts=len(parts), final=final),
        grid=(t // tm,),
        in_specs=in_specs,
        out_specs=row,
        out_shape=jax.ShapeDtypeStruct((t, D_MODEL), F32),
        compiler_params=_params("parallel"),
        name=("ffn_mix" if parts else "ffn") + ("_final" if final else ""),
    )(*args)


def _proj(hn, w_ref, start, width):
    return jnp.dot(hn, w_ref[:, start:start + width], preferred_element_type=F32)


HEAD_PAIR = 2 * LANES


def _iota2(shape, axis):
    return lax.broadcasted_iota(jnp.int32, shape, axis)


def _hgrn2_kernel(x_ref, mnw_ref, w_ref, lbl_ref, nw_ref, o_ref, u_ref, st_ref, *, layer, tl):
    nc = tl // CHUNK
    hn = _rmsnorm(x_ref[...], mnw_ref[...]).astype(BF16)
    u_ref[...] = _proj(hn, w_ref, 2 * HG_QK + 2 * HG_WIDTH, S5_WIDTH)

    @pl.when(pl.program_id(1) == 0)
    def _():
        st_ref[...] = jnp.zeros(st_ref.shape, F32)

    logits = lbl_ref[...]
    e = jnp.exp(logits - jnp.max(logits, axis=0, keepdims=True))
    p = e / jnp.sum(e, axis=0, keepdims=True)
    lb = jnp.sum(p[0:layer + 1], axis=0, keepdims=True) - p[0:1]
    nw = nw_ref[...]

    rt = _iota2((tl, tl), 0)
    ct = _iota2((tl, tl), 1)
    chunk_lower = ((rt // CHUNK) == (ct // CHUNK)) & (rt >= ct)
    levels = (32, 16, 8, 4)
    level_mask = {m: ((rt // (2 * m)) == (ct // (2 * m))) & ((rt // m) % 2 == 1) & ((ct // m) % 2 == 0)
                  for m in levels}
    base = 4
    base_mask = ((rt // base) == (ct // base)) & (rt >= ct)
    pos = _iota2((tl, 1), 0) % base

    pairs = []
    for pr in range(HG_QK // HEAD_PAIR):
        c0 = pr * HEAD_PAIR
        lbp = lb[:, c0:c0 + HEAD_PAIR]
        q = _silu(_proj(hn, w_ref, c0, HEAD_PAIR))
        f = lbp + (1.0 - lbp) * _sigmoid(_proj(hn, w_ref, HG_QK + c0, HEAD_PAIR))
        lf = jnp.log(jnp.maximum(f, F_MIN))
        v = _proj(hn, w_ref, 2 * HG_QK + c0, HEAD_PAIR)
        gate = _silu(_proj(hn, w_ref, 2 * HG_QK + HG_WIDTH + c0, HEAD_PAIR))
        b = _mm_sel(chunk_lower, lf)
        c4 = lf
        for j in range(1, base):
            c4 = c4 + jnp.where(pos >= j, pltpu.roll(lf, j, 0), 0.0)
        pairs.append((b, q, 1.0 - f, v, c4, gate))

    outs = []
    for h in range(HG_HEADS):
        sl = slice(h * HG_DK % HEAD_PAIR, h * HG_DK % HEAD_PAIR + HG_DK)
        bh, qh, kh, vh, ch, gh = (a[:, sl] for a in pairs[h * HG_DK // HEAD_PAIR])
        attn = jnp.where(base_mask, _mm_nt(qh * jnp.exp(ch), kh * jnp.exp(-ch)), 0.0)
        for m in levels:
            if m == base:
                b3 = bh.reshape(tl // (2 * m), 2 * m, HG_DK)
                ref_q = ref_k = b3[:, m - 1:m, :]
            else:
                b3 = bh.reshape(tl // m, m, HG_DK)
                ref_k = b3[:, m - 1:m, :]
                ref_q = jnp.concatenate([jnp.zeros((1, 1, HG_DK), F32), ref_k[:-1]], axis=0)
            qt = qh * jnp.exp(jnp.minimum(b3 - ref_q, 0.0)).reshape(tl, HG_DK)
            kt = kh * jnp.exp(jnp.minimum(ref_k - b3, 0.0)).reshape(tl, HG_DK)
            attn = jnp.where(level_mask[m], _mm_nt(qt, kt), attn)
        o = _mm(attn, vh)
        bc = bh.reshape(nc, CHUNK, HG_DK)
        blast = bc[:, CHUNK - 1:CHUNK, :]
        ks = kh * jnp.exp(blast - bc).reshape(tl, HG_DK)
        qe = qh * jnp.exp(bh)
        st = st_ref[h]
        o_state = []
        for c in range(nc):
            rows = slice(c * CHUNK, (c + 1) * CHUNK)
            o_state.append(_mm_nt(qe[rows], st))
            st = st * jnp.exp(blast[c]) + _mm_tn(vh[rows], ks[rows])
        st_ref[h] = st
        o = o + jnp.concatenate(o_state, axis=0)
        o = o * lax.rsqrt(jnp.mean(o * o, axis=-1, keepdims=True) + NORM_EPS) * nw
        outs.append(o * gh)
    o_ref[...] = jnp.concatenate(outs, axis=-1)


def hgrn2(x, mix_norm_w, w_in, lb_logits, norm_w, layer, batch, seq):
    t = x.shape[0]
    tl = min(MIX_TILE, seq)
    steps = seq // tl
    n_layers = lb_logits.shape[0]
    return pl.pallas_call(
        functools.partial(_hgrn2_kernel, layer=layer, tl=tl),
        grid=(batch, steps),
        in_specs=[pl.BlockSpec((tl, D_MODEL), lambda b, j: (b * steps + j, 0)), _resident((1, D_MODEL)),
                  _resident(w_in.shape), _resident((n_layers, HG_QK)), _resident((1, HG_DV))],
        out_specs=[pl.BlockSpec((tl, HG_WIDTH), lambda b, j: (b * steps + j, 0)),
                   pl.BlockSpec((tl, S5_WIDTH), lambda b, j: (j, b))],
        out_shape=[jax.ShapeDtypeStruct((t, HG_WIDTH), F32),
                   jax.ShapeDtypeStruct((seq, batch * S5_WIDTH), F32)],
        scratch_shapes=[pltpu.VMEM((HG_HEADS, HG_DV, HG_DK), F32)],
        compiler_params=_params("parallel", "arbitrary"),
        name="hgrn2",
    )(x, mix_norm_w.reshape(1, D_MODEL), w_in.astype(BF16), lb_logits, norm_w.reshape(1, HG_DV))


def _s5_prep_kernel(are_ref, aim_ref, ldt_ref, bre_ref, bim_ref, abr_ref, abi_ref, bbr_ref, bbi_ref):
    a_re, a_im = are_ref[...], aim_ref[...]
    dt = jnp.exp(ldt_ref[...])
    mag = jnp.exp(dt * a_re)
    ang = dt * a_im
    abar_re = mag * jnp.cos(ang)
    abar_im = mag * jnp.sin(ang)
    den = a_re * a_re + a_im * a_im
    zr = abar_re - 1.0
    zi = abar_im
    coef_re = (zr * a_re + zi * a_im) / den
    coef_im = (zi * a_re - zr * a_im) / den
    b_re, b_im = bre_ref[...], bim_ref[...]
    abr_ref[...] = abar_re
    abi_ref[...] = abar_im
    bbr_ref[...] = coef_re * b_re - coef_im * b_im
    bbi_ref[...] = coef_re * b_im + coef_im * b_re


def s5_prep(a_re, a_im, log_dt, b_re, b_im):
    flat = lambda a: a.reshape(1, S5_NSTATE)
    ldt = jnp.broadcast_to(log_dt[:, None], (S5_GROUPS, S5_STATE))
    by_p = lambda b: b.transpose(2, 0, 1).reshape(S5_GROUP, S5_NSTATE)
    row = jax.ShapeDtypeStruct((1, S5_NSTATE), F32)
    mat = jax.ShapeDtypeStruct((S5_GROUP, S5_NSTATE), F32)
    return pl.pallas_call(_s5_prep_kernel, out_shape=[row, row, mat, mat], name="s5_prep")(
        flat(a_re), flat(a_im), flat(ldt), by_p(b_re), by_p(b_im))


def _s5_kernel(u_ref, abr_ref, abi_ref, bre_ref, bim_ref, cre_ref, cim_ref, d_ref, wglu_ref, o_ref,
               hre_ref, him_ref, sre_ref, sim_ref, y_ref, *, tt, nb):
    @pl.when(pl.program_id(0) == 0)
    def _():
        hre_ref[...] = jnp.zeros(hre_ref.shape, F32)
        him_ref[...] = jnp.zeros(him_ref.shape, F32)

    u = u_ref[...].reshape(tt * nb, S5_WIDTH)
    for j in range(S5_WIDTH // S5_LANE_CHUNK):
        uj = u[:, j * S5_LANE_CHUNK:(j + 1) * S5_LANE_CHUNK].astype(BF16)
        ssl = slice(j * S5_STATE_CHUNK, (j + 1) * S5_STATE_CHUNK)
        sre_ref[...] = jnp.dot(uj, bre_ref[j], preferred_element_type=F32)
        sim_ref[...] = jnp.dot(uj, bim_ref[j], preferred_element_type=F32)
        ar = abr_ref[:, ssl]
        ai = abi_ref[:, ssl]

        def step(t, carry):
            hr, hi = carry
            rows = pl.ds(pl.multiple_of(t * nb, nb), nb)
            nr = ar * hr - ai * hi + sre_ref[rows, :]
            ni = ar * hi + ai * hr + sim_ref[rows, :]
            sre_ref[rows, :] = nr
            sim_ref[rows, :] = ni
            return nr, ni

        hr, hi = lax.fori_loop(0, tt, step, (hre_ref[:, ssl], him_ref[:, ssl]), unroll=4)
        hre_ref[:, ssl] = hr
        him_ref[:, ssl] = hi
        y_ref[:, j * S5_LANE_CHUNK:(j + 1) * S5_LANE_CHUNK] = (
            jnp.dot(sre_ref[...].astype(BF16), cre_ref[j], preferred_element_type=F32)
            - jnp.dot(sim_ref[...].astype(BF16), cim_ref[j], preferred_element_type=F32))
    y = jax.nn.gelu(y_ref[...] + d_ref[...] * u)
    out = y * _sigmoid(jnp.dot(y.astype(BF16), wglu_ref[...], preferred_element_type=F32))
    o_ref[...] = out.reshape(tt, nb, S5_WIDTH)


def _group_block_diag(m):
    per = S5_LANE_CHUNK // S5_GROUP
    g, r, c = m.shape
    m = m.reshape(g // per, per, r, c)
    eye = jnp.eye(per, dtype=m.dtype)
    return jnp.einsum("jgrc,gh->jgrhc", m, eye).reshape(g // per, per * r, per * c)


def s5(u_tm, a_re, a_im, b_re, b_im, c_re, c_im, d, log_dt, w_glu, batch, seq):
    abar_re, abar_im, bb_re, bb_im = s5_prep(a_re, a_im, log_dt, b_re, b_im)
    to_groups = lambda bb: bb.reshape(S5_GROUP, S5_GROUPS, S5_STATE).transpose(1, 0, 2)
    bre = _group_block_diag(to_groups(bb_re)).astype(BF16)
    bim = _group_block_diag(to_groups(bb_im)).astype(BF16)
    cre = _group_block_diag(c_re.transpose(0, 2, 1)).astype(BF16)
    cim = _group_block_diag(c_im.transpose(0, 2, 1)).astype(BF16)
    tt = min(S5_TILE, seq)
    u3 = u_tm.reshape(seq, batch, S5_WIDTH)
    blk = pl.BlockSpec((tt, batch, S5_WIDTH), lambda i: (i, 0, 0))
    rows = tt * batch
    out = pl.pallas_call(
        functools.partial(_s5_kernel, tt=tt, nb=batch),
        grid=(seq // tt,),
        in_specs=[blk, _resident((1, S5_NSTATE)), _resident((1, S5_NSTATE)),
                  _resident(bre.shape), _resident(bim.shape), _resident(cre.shape),
                  _resident(cim.shape), _resident((1, S5_WIDTH)), _resident((S5_WIDTH, S5_WIDTH))],
        out_specs=blk,
        out_shape=jax.ShapeDtypeStruct((seq, batch, S5_WIDTH), F32),
        scratch_shapes=[pltpu.VMEM((batch, S5_NSTATE), F32), pltpu.VMEM((batch, S5_NSTATE), F32),
                        pltpu.VMEM((rows, S5_STATE_CHUNK), F32), pltpu.VMEM((rows, S5_STATE_CHUNK), F32),
                        pltpu.VMEM((rows, S5_WIDTH), F32)],
        compiler_params=_params("arbitrary"),
        name="s5",
    )(u3, abar_re, abar_im, bre, bim, cre, cim, d.reshape(1, S5_WIDTH), w_glu.astype(BF16))
    return out.reshape(seq, batch * S5_WIDTH)


def _softplus(x):
    return jnp.maximum(x, 0.0) + jnp.log(1.0 + jnp.exp(-jnp.abs(x)))


def _mm_sel(sel, x):
    x1 = x.astype(BF16)
    r1 = x - x1.astype(F32)
    x2 = r1.astype(BF16)
    x3 = (r1 - x2.astype(F32)).astype(BF16)
    s = sel.astype(BF16)
    return jnp.dot(jnp.concatenate([s, s, s], axis=1), jnp.concatenate([x1, x2, x3], axis=0),
                   preferred_element_type=F32)


def _gdn_kernel(x_ref, mnw_ref, w_ref, cw_ref, alog_ref, dtb_ref, nw_ref, o_ref, s_ref, tail_ref, *, tl):
    nc = tl // CHUNK

    @pl.when(pl.program_id(1) == 0)
    def _():
        s_ref[...] = jnp.zeros(s_ref.shape, F32)
        tail_ref[...] = jnp.zeros(tail_ref.shape, F32)

    hn = _rmsnorm(x_ref[...], mnw_ref[...]).astype(BF16)
    row_id = _iota2((tl, 1), 0)

    def shift_rows(z, carry_row):
        return jnp.where(row_id == 0, carry_row, pltpu.roll(z, 1, 0))

    def conv_silu(idx, c0):
        x = _proj(hn, w_ref, idx * GDN_QK + c0, HEAD_PAIR)
        w = cw_ref[idx, :, c0:c0 + HEAD_PAIR]
        t = tail_ref[idx, :, c0:c0 + HEAD_PAIR]
        tail_ref[idx, :, c0:c0 + HEAD_PAIR] = x[tl - SUBLANES:tl, :]
        xm1, xm2, xm3 = (t[SUBLANES - i:SUBLANES - i + 1] for i in (1, 2, 3))
        w0, w1, w2, w3 = (w[j:j + 1] for j in range(CONV_W))
        z = w0 * x
        z = w1 * x + shift_rows(z, w0 * xm1)
        z = w2 * x + shift_rows(z, w1 * xm1 + w0 * xm2)
        z = w3 * x + shift_rows(z, w2 * xm1 + w1 * xm2 + w0 * xm3)
        return _silu(z)

    pairs = []
    for pr in range(GDN_QK // HEAD_PAIR):
        c0 = pr * HEAD_PAIR
        pairs.append((conv_silu(0, c0), conv_silu(1, c0), conv_silu(2, c0),
                      _silu(_proj(hn, w_ref, 3 * GDN_QK + c0, HEAD_PAIR))))

    rt = _iota2((tl, tl), 0)
    ct = _iota2((tl, tl), 1)
    same = (rt // CHUNK) == (ct // CHUNK)
    lower = same & (rt >= ct)
    strict = same & (rt > ct)
    rs = _iota2((CHUNK, tl), 0)
    cs = _iota2((CHUNK, tl), 1)
    eye_side = (rs == cs % CHUNK).astype(F32)
    side_block = cs // CHUNK

    def to_diag(side):
        return jnp.where(same, jnp.concatenate([side] * nc, axis=0), 0.0)

    def from_diag(full):
        out = jnp.zeros((CHUNK, tl), F32)
        for c in range(nc):
            out = out + jnp.where(side_block == c, full[c * CHUNK:(c + 1) * CHUNK, :], 0.0)
        return out

    bd = _proj(hn, w_ref, 4 * GDN_QK, LANES)
    beta_all = _sigmoid(bd)
    la_all = -jnp.exp(alog_ref[...]) * _softplus(bd + dtb_ref[...])
    g_all = _mm_sel(lower, la_all)
    g_t = g_all.T
    nw = nw_ref[...]

    heads = range(GDN_HEADS)
    kn, qg, vkb, attn, gcols, n_side, t_side, gates = [], [], [], [], [], [], [], []
    for h in heads:
        sl = slice(h * GDN_DK % HEAD_PAIR, h * GDN_DK % HEAD_PAIR + GDN_DK)
        qh, kh, vh, gh = (a[:, sl] for a in pairs[h * GDN_DK // HEAD_PAIR])
        gates.append(gh)
        qh = qh * lax.rsqrt(jnp.sum(qh * qh, axis=-1, keepdims=True) + NORM_EPS) * (GDN_DK ** -0.5)
        kh = kh * lax.rsqrt(jnp.sum(kh * kh, axis=-1, keepdims=True) + NORM_EPS)
        beta = beta_all[:, h:h + 1]
        gcol = g_all[:, GDN_HEADS + h:GDN_HEADS + h + 1]
        grow = g_t[GDN_HEADS + h:GDN_HEADS + h + 1, :]
        lmask = jnp.where(lower, jnp.exp(jnp.minimum(gcol - grow, 0.0)), 0.0)
        kb = kh * beta
        kq = _mm_nt(jnp.concatenate([kb, qh], axis=0), kh)
        m = jnp.where(strict, kq[:tl] * lmask, 0.0)
        attn.append(kq[tl:] * lmask)
        eg = jnp.exp(gcol)
        n = -from_diag(m)
        n_side.append(n)
        t_side.append(eye_side + n)
        vkb.append(jnp.concatenate([vh * beta, kb * eg], axis=1))
        kn.append(kh)
        qg.append(qh * eg)
        gcols.append(gcol)

    for h in heads:
        n_side[h] = _mm(n_side[h], to_diag(n_side[h]))
    for _ in range(4):
        for h in heads:
            r = _mm(jnp.concatenate([t_side[h], n_side[h]], axis=0), to_diag(n_side[h]))
            t_side[h] = t_side[h] + r[:CHUNK]
            n_side[h] = r[CHUNK:]
    uw = []
    for h in heads:
        t_inv = t_side[h] + _mm(t_side[h], to_diag(n_side[h]))
        uw.append(_mm(to_diag(t_inv), vkb[h]))

    s = [s_ref[h] for h in heads]
    v_new = [[] for _ in heads]
    o_state = [[] for _ in heads]
    for c in range(nc):
        rows = slice(c * CHUNK, (c + 1) * CHUNK)
        ws = [_mm(jnp.concatenate([uw[h][rows, GDN_DV:], qg[h][rows]], axis=0), s[h]) for h in heads]
        for h in heads:
            vn = uw[h][rows, :GDN_DV] - ws[h][:CHUNK]
            v_new[h].append(vn)
            o_state[h].append(ws[h][CHUNK:])
            gc = gcols[h][rows]
            glast = gc[CHUNK - 1:CHUNK, :]
            s[h] = s[h] * jnp.exp(glast) + _mm_tn(kn[h][rows] * jnp.exp(glast - gc), vn)
    for h in heads:
        s_ref[h] = s[h]
        sl = slice(h * GDN_DK, (h + 1) * GDN_DK)
        o = jnp.concatenate(o_state[h], axis=0) + _mm(attn[h], jnp.concatenate(v_new[h], axis=0))
        o = o * lax.rsqrt(jnp.mean(o * o, axis=-1, keepdims=True) + NORM_EPS) * nw
        o_ref[:, sl] = o * gates[h]


def gdn(x, mix_norm_w, w_in, conv_w, a_log, dt_bias, norm_w, batch, seq):
    t = x.shape[0]
    tl = min(MIX_TILE, seq)
    steps = seq // tl
    tile = lambda n: pl.BlockSpec((tl, n), lambda b, j: (b * steps + j, 0))
    n_main = 2 * GDN_QK + 2 * GDN_V
    w_in = jnp.pad(w_in, ((0, 0), (0, LANES - (w_in.shape[1] - n_main))))
    cw = conv_w.reshape(CONV_W, 3, GDN_QK).transpose(1, 0, 2)
    lane_row = lambda p: jnp.zeros((1, LANES), F32).at[0, GDN_HEADS:2 * GDN_HEADS].set(p)
    return pl.pallas_call(
        functools.partial(_gdn_kernel, tl=tl),
        grid=(batch, steps),
        in_specs=[tile(D_MODEL), _resident((1, D_MODEL)), _resident(w_in.shape),
                  _resident((3, CONV_W, GDN_QK)), _resident((1, LANES)), _resident((1, LANES)),
                  _resident((1, GDN_DV))],
        out_specs=tile(GDN_V),
        out_shape=jax.ShapeDtypeStruct((t, GDN_V), F32),
        scratch_shapes=[pltpu.VMEM((GDN_HEADS, GDN_DK, GDN_DV), F32),
                        pltpu.VMEM((3, SUBLANES, GDN_QK), F32)],
        compiler_params=_params("parallel", "arbitrary"),
        name="gdn",
    )(x, mix_norm_w.reshape(1, D_MODEL), w_in.astype(BF16), cw, lane_row(a_log), lane_row(dt_bias),
      norm_w.reshape(1, GDN_DV))


def kernel(x, ffn1_norm, ffn1_w_gate, ffn1_w_up, ffn1_w_down, mix_norm, ffn2_norm, ffn2_w_gate, ffn2_w_up, ffn2_w_down, ev_w_in, hg_lb_logits, hg_norm_w, s5_a_re, s5_a_im, s5_b_re, s5_b_im, s5_c_re, s5_c_im, s5_d, s5_log_dt, s5_w_glu, ev_w_out, od_w_in, gdn_conv_w, gdn_a_log, gdn_dt_bias, gdn_norm_w, od_w_out, final_norm):
    batch, seq, _ = x.shape
    xf = x.reshape(batch * seq, D_MODEL)
    ffn1_w = [w.astype(BF16) for w in (ffn1_w_gate, ffn1_w_up, ffn1_w_down)]
    ffn2_w = [w.astype(BF16) for w in (ffn2_w_gate, ffn2_w_up, ffn2_w_down)]
    for layer in range(DEPTH):
        xf = ffn(xf, ffn1_norm[layer], *ffn1_w, layer, seq)
        j = layer // 2
        if layer % 2 == 0:
            y_a, u_tm = hgrn2(xf, mix_norm[layer], ev_w_in[j], hg_lb_logits, hg_norm_w[j], j, batch, seq)
            y_b = s5(u_tm, s5_a_re[j], s5_a_im[j], s5_b_re[j], s5_b_im[j], s5_c_re[j], s5_c_im[j],
                     s5_d[j], s5_log_dt[j], s5_w_glu[j], batch, seq)
            w_out = ev_w_out[j]
            parts = [(y_a, w_out[:HG_WIDTH], False), (y_b, w_out[HG_WIDTH:], True)]
        else:
            y = gdn(xf, mix_norm[layer], od_w_in[j], gdn_conv_w[j], gdn_a_log[j], gdn_dt_bias[j],
                    gdn_norm_w[j], batch, seq)
            parts = [(y, od_w_out[j], False)]
        last = layer == DEPTH - 1
        xf = ffn(xf, ffn2_norm[layer], *ffn2_w, layer, seq, parts=parts,
                 final_w=final_norm if last else None)
    return xf.reshape(batch, seq, D_MODEL)
```

```python
import functools

import jax
import jax.numpy as jnp
from jax import lax
from jax.experimental import pallas as pl
from jax.experimental.pallas import tpu as pltpu

F32 = jnp.float32
BF16 = jnp.bfloat16

D_MODEL = 1024
DEPTH = 4
D_FF = 2816
NORM_EPS = 1e-6
F_MIN = 1e-6
CHUNK = 64
HG_HEADS = 4
HG_DK = 128
HG_DV = 128
HG_QK = HG_HEADS * HG_DK
HG_WIDTH = HG_HEADS * HG_DV
S5_WIDTH = D_MODEL - HG_WIDTH
S5_GROUP = 16
S5_GROUPS = S5_WIDTH // S5_GROUP
S5_STATE = 64
S5_NSTATE = S5_GROUPS * S5_STATE
GDN_HEADS = 8
GDN_DK = 128
GDN_DV = 128
GDN_QK = GDN_HEADS * GDN_DK
GDN_V = GDN_HEADS * GDN_DV
CONV_W = 4

LANES = 128
SUBLANES = 8
VMEM_LIMIT_BYTES = 56 * 1024 * 1024

TOKEN_TILE = 512
FF_BLOCK = 256
MIX_TILE = 256
S5_TILE = 64
S5_LANE_CHUNK = 128
S5_STATE_CHUNK = S5_LANE_CHUNK // S5_GROUP * S5_STATE


def _mm(a, b):
    return jnp.dot(a.astype(BF16), b.astype(BF16), preferred_element_type=F32)


def _mm_nt(a, b):
    return lax.dot_general(a.astype(BF16), b.astype(BF16), (((1,), (1,)), ((), ())),
                           preferred_element_type=F32)


def _mm_tn(a, b):
    return lax.dot_general(a.astype(BF16), b.astype(BF16), (((0,), (0,)), ((), ())),
                           preferred_element_type=F32)


def _sigmoid(x):
    return 1.0 / (1.0 + jnp.exp(-x))


def _silu(x):
    return x * _sigmoid(x)


def _rmsnorm(x, w):
    return x * lax.rsqrt(jnp.mean(x * x, axis=-1, keepdims=True) + NORM_EPS) * w


def _resident(shape):
    nd = len(shape)
    return pl.BlockSpec(shape, lambda *_: (0,) * nd, pipeline_mode=pl.Buffered(1))


def _params(*sem):
    return pltpu.CompilerParams(dimension_semantics=sem, vmem_limit_bytes=VMEM_LIMIT_BYTES)


def _ffn_kernel(*refs, n_parts, final):
    x_ref, refs = refs[0], refs[1:]
    part_refs, refs = refs[:2 * n_parts], refs[2 * n_parts:]
    if final:
        nw_ref, wg_ref, wu_ref, wd_ref, fw_ref, o_ref = refs
    else:
        nw_ref, wg_ref, wu_ref, wd_ref, o_ref = refs
    x = x_ref[...]
    for y_ref, w_ref in zip(part_refs[0::2], part_refs[1::2]):
        x = x + jnp.dot(y_ref[...].astype(BF16), w_ref[...], preferred_element_type=F32)
    h = _rmsnorm(x, nw_ref[...])
    acc = jnp.zeros(x.shape, F32)
    for j in range(D_FF // FF_BLOCK):
        sl = slice(j * FF_BLOCK, (j + 1) * FF_BLOCK)
        g = jnp.dot(h, wg_ref[:, sl], preferred_element_type=F32)
        u = jnp.dot(h, wu_ref[:, sl], preferred_element_type=F32)
        a = _silu(g) * u
        acc = acc + jnp.dot(a, wd_ref[sl, :], preferred_element_type=F32)
    y = x + 0.5 * acc
    if final:
        y = _rmsnorm(y, fw_ref[...])
    o_ref[...] = y


def _resident_layer(shape, layer):
    nd = len(shape) - 1
    return pl.BlockSpec((None,) + tuple(shape[1:]), lambda *_: (layer,) + (0,) * nd,
                        pipeline_mode=pl.Buffered(1))


def ffn(x, norm_w, w_gate, w_up, w_down, layer, seq, parts=(), final_w=None):
    t = x.shape[0]
    tm = min(TOKEN_TILE, seq)
    blocks_per_seq = seq // tm
    final = final_w is not None
    row = pl.BlockSpec((tm, D_MODEL), lambda i: (i, 0))
    in_specs, args = [row], [x]
    for y, w, time_major in parts:
        n = w.shape[0]
        if time_major:
            in_specs.append(pl.BlockSpec(
                (tm, n), lambda i: (i % blocks_per_seq, i // blocks_per_seq)))
        else:
            in_specs.append(pl.BlockSpec((tm, n), lambda i: (i, 0)))
        in_specs.append(_resident((n, D_MODEL)))
        args += [y, w.astype(BF16)]
    in_specs += [_resident((1, D_MODEL))] + [_resident_layer(w.shape, layer) for w in (w_gate, w_up, w_down)]
    args += [norm_w.reshape(1, D_MODEL), w_gate, w_up, w_down]
    if final:
        in_specs.append(_resident((1, D_MODEL)))
        args.append(final_w.reshape(1, D_MODEL))
    return pl.pallas_call(
        functools.partial(_ffn_kernel, n_parts=len(parts), final=final),
        grid=(t // tm,),
        in_specs=in_specs,
        out_specs=row,
        out_shape=jax.ShapeDtypeStruct((t, D_MODEL), F32),
        compiler_params=_params("parallel"),
        name=("ffn_mix" if parts else "ffn") + ("_final" if final else ""),
    )(*args)


def _proj(hn, w_ref, start, width):
    return jnp.dot(hn, w_ref[:, start:start + width], preferred_element_type=F32)


HEAD_PAIR = 2 * LANES


def _iota2(shape, axis):
    return lax.broadcasted_iota(jnp.int32, shape, axis)


def _hgrn2_kernel(x_ref, mnw_ref, w_ref, lbl_ref, nw_ref, o_ref, u_ref, st_ref, *, layer, tl):
    nc = tl // CHUNK
    hn = _rmsnorm(x_ref[...], mnw_ref[...]).astype(BF16)
    u_ref[...] = _proj(hn, w_ref, 2 * HG_QK + 2 * HG_WIDTH, S5_WIDTH)

    @pl.when(pl.program_id(1) == 0)
    def _():
        st_ref[...] = jnp.zeros(st_ref.shape, F32)

    logits = lbl_ref[...]
    e = jnp.exp(logits - jnp.max(logits, axis=0, keepdims=True))
    p = e / jnp.sum(e, axis=0, keepdims=True)
    lb = jnp.sum(p[0:layer + 1], axis=0, keepdims=True) - p[0:1]
    nw = nw_ref[...]

    rt = _iota2((tl, tl), 0)
    ct = _iota2((tl, tl), 1)
    chunk_lower = ((rt // CHUNK) == (ct // CHUNK)) & (rt >= ct)
    levels = (32, 16, 8, 4)
    level_mask = {m: ((rt // (2 * m)) == (ct // (2 * m))) & ((rt // m) % 2 == 1) & ((ct // m) % 2 == 0)
                  for m in levels}
    base = 4
    base_mask = ((rt // base) == (ct // base)) & (rt >= ct)
    pos = _iota2((tl, 1), 0) % base

    pairs = []
    for pr in range(HG_QK // HEAD_PAIR):
        c0 = pr * HEAD_PAIR
        lbp = lb[:, c0:c0 + HEAD_PAIR]
        q = _silu(_proj(hn, w_ref, c0, HEAD_PAIR))
        f = lbp + (1.0 - lbp) * _sigmoid(_proj(hn, w_ref, HG_QK + c0, HEAD_PAIR))
        lf = jnp.log(jnp.maximum(f, F_MIN))
        v = _proj(hn, w_ref, 2 * HG_QK + c0, HEAD_PAIR)
        gate = _silu(_proj(hn, w_ref, 2 * HG_QK + HG_WIDTH + c0, HEAD_PAIR))
        b = _mm_sel(chunk_lower, lf)
        c4 = lf
        for j in range(1, base):
            c4 = c4 + jnp.where(pos >= j, pltpu.roll(lf, j, 0), 0.0)
        pairs.append((b, q, 1.0 - f, v, c4, gate))

    outs = []
    for h in range(HG_HEADS):
        sl = slice(h * HG_DK % HEAD_PAIR, h * HG_DK % HEAD_PAIR + HG_DK)
        bh, qh, kh, vh, ch, gh = (a[:, sl] for a in pairs[h * HG_DK // HEAD_PAIR])
        attn = jnp.where(base_mask, _mm_nt(qh * jnp.exp(ch), kh * jnp.exp(-ch)), 0.0)
        for m in levels:
            if m == base:
                b3 = bh.reshape(tl // (2 * m), 2 * m, HG_DK)
                ref_q = ref_k = b3[:, m - 1:m, :]
            else:
                b3 = bh.reshape(tl // m, m, HG_DK)
                ref_k = b3[:, m - 1:m, :]
                ref_q = jnp.concatenate([jnp.zeros((1, 1, HG_DK), F32), ref_k[:-1]], axis=0)
            qt = qh * jnp.exp(jnp.minimum(b3 - ref_q, 0.0)).reshape(tl, HG_DK)
            kt = kh * jnp.exp(jnp.minimum(ref_k - b3, 0.0)).reshape(tl, HG_DK)
            attn = jnp.where(level_mask[m], _mm_nt(qt, kt), attn)
        o = _mm(attn, vh)
        bc = bh.reshape(nc, CHUNK, HG_DK)
        blast = bc[:, CHUNK - 1:CHUNK, :]
        ks = kh * jnp.exp(blast - bc).reshape(tl, HG_DK)
        qe = qh * jnp.exp(bh)
        st = st_ref[h]
        o_state = []
        for c in range(nc):
            rows = slice(c * CHUNK, (c + 1) * CHUNK)
            o_state.append(_mm_nt(qe[rows], st))
            st = st * jnp.exp(blast[c]) + _mm_tn(vh[rows], ks[rows])
        st_ref[h] = st
        o = o + jnp.concatenate(o_state, axis=0)
        o = o * lax.rsqrt(jnp.mean(o * o, axis=-1, keepdims=True) + NORM_EPS) * nw
        outs.append(o * gh)
    o_ref[...] = jnp.concatenate(outs, axis=-1)


def hgrn2(x, mix_norm_w, w_in, lb_logits, norm_w, layer, batch, seq):
    t = x.shape[0]
    tl = min(MIX_TILE, seq)
    steps = seq // tl
    n_layers = lb_logits.shape[0]
    return pl.pallas_call(
        functools.partial(_hgrn2_kernel, layer=layer, tl=tl),
        grid=(batch, steps),
        in_specs=[pl.BlockSpec((tl, D_MODEL), lambda b, j: (b * steps + j, 0)), _resident((1, D_MODEL)),
                  _resident(w_in.shape), _resident((n_layers, HG_QK)), _resident((1, HG_DV))],
        out_specs=[pl.BlockSpec((tl, HG_WIDTH), lambda b, j: (b * steps + j, 0)),
                   pl.BlockSpec((tl, S5_WIDTH), lambda b, j: (j, b))],
        out_shape=[jax.ShapeDtypeStruct((t, HG_WIDTH), F32),
                   jax.ShapeDtypeStruct((seq, batch * S5_WIDTH), F32)],
        scratch_shapes=[pltpu.VMEM((HG_HEADS, HG_DV, HG_DK), F32)],
        compiler_params=_params("parallel", "arbitrary"),
        name="hgrn2",
    )(x, mix_norm_w.reshape(1, D_MODEL), w_in.astype(BF16), lb_logits, norm_w.reshape(1, HG_DV))


def _s5_prep_kernel(are_ref, aim_ref, ldt_ref, bre_ref, bim_ref, abr_ref, abi_ref, bbr_ref, bbi_ref):
    a_re, a_im = are_ref[...], aim_ref[...]
    dt = jnp.exp(ldt_ref[...])
    mag = jnp.exp(dt * a_re)
    ang = dt * a_im
    abar_re = mag * jnp.cos(ang)
    abar_im = mag * jnp.sin(ang)
    den = a_re * a_re + a_im * a_im
    zr = abar_re - 1.0
    zi = abar_im
    coef_re = (zr * a_re + zi * a_im) / den
    coef_im = (zi * a_re - zr * a_im) / den
    b_re, b_im = bre_ref[...], bim_ref[...]
    abr_ref[...] = abar_re
    abi_ref[...] = abar_im
    bbr_ref[...] = coef_re * b_re - coef_im * b_im
    bbi_ref[...] = coef_re * b_im + coef_im * b_re


def s5_prep(a_re, a_im, log_dt, b_re, b_im):
    flat = lambda a: a.reshape(1, S5_NSTATE)
    ldt = jnp.broadcast_to(log_dt[:, None], (S5_GROUPS, S5_STATE))
    by_p = lambda b: b.transpose(2, 0, 1).reshape(S5_GROUP, S5_NSTATE)
    row = jax.ShapeDtypeStruct((1, S5_NSTATE), F32)
    mat = jax.ShapeDtypeStruct((S5_GROUP, S5_NSTATE), F32)
    return pl.pallas_call(_s5_prep_kernel, out_shape=[row, row, mat, mat], name="s5_prep")(
        flat(a_re), flat(a_im), flat(ldt), by_p(b_re), by_p(b_im))


def _s5_kernel(u_ref, abr_ref, abi_ref, bre_ref, bim_ref, cre_ref, cim_ref, d_ref, wglu_ref, o_ref,
               hre_ref, him_ref, sre_ref, sim_ref, y_ref, *, tt, nb):
    @pl.when(pl.program_id(0) == 0)
    def _():
        hre_ref[...] = jnp.zeros(hre_ref.shape, F32)
        him_ref[...] = jnp.zeros(him_ref.shape, F32)

    u = u_ref[...].reshape(tt * nb, S5_WIDTH)
    for j in range(S5_WIDTH // S5_LANE_CHUNK):
        uj = u[:, j * S5_LANE_CHUNK:(j + 1) * S5_LANE_CHUNK].astype(BF16)
        ssl = slice(j * S5_STATE_CHUNK, (j + 1) * S5_STATE_CHUNK)
        sre_ref[...] = jnp.dot(uj, bre_ref[j], preferred_element_type=F32)
        sim_ref[...] = jnp.dot(uj, bim_ref[j], preferred_element_type=F32)
        ar = abr_ref[:, ssl]
        ai = abi_ref[:, ssl]

        def step(t, carry):
            hr, hi = carry
            rows = pl.ds(pl.multiple_of(t * nb, nb), nb)
            nr = ar * hr - ai * hi + sre_ref[rows, :]
            ni = ar * hi + ai * hr + sim_ref[rows, :]
            sre_ref[rows, :] = nr
            sim_ref[rows, :] = ni
            return nr, ni

        hr, hi = lax.fori_loop(0, tt, step, (hre_ref[:, ssl], him_ref[:, ssl]), unroll=4)
        hre_ref[:, ssl] = hr
        him_ref[:, ssl] = hi
        y_ref[:, j * S5_LANE_CHUNK:(j + 1) * S5_LANE_CHUNK] = (
            jnp.dot(sre_ref[...].astype(BF16), cre_ref[j], preferred_element_type=F32)
            - jnp.dot(sim_ref[...].astype(BF16), cim_ref[j], preferred_element_type=F32))
    y = jax.nn.gelu(y_ref[...] + d_ref[...] * u)
    out = y * _sigmoid(jnp.dot(y.astype(BF16), wglu_ref[...], preferred_element_type=F32))
    o_ref[...] = out.reshape(tt, nb, S5_WIDTH)


def _group_block_diag(m):
    per = S5_LANE_CHUNK // S5_GROUP
    g, r, c = m.shape
    m = m.reshape(g // per, per, r, c)
    eye = jnp.eye(per, dtype=m.dtype)
    return jnp.einsum("jgrc,gh->jgrhc", m, eye).reshape(g // per, per * r, per * c)


def s5(u_tm, a_re, a_im, b_re, b_im, c_re, c_im, d, log_dt, w_glu, batch, seq):
    abar_re, abar_im, bb_re, bb_im = s5_prep(a_re, a_im, log_dt, b_re, b_im)
    to_groups = lambda bb: bb.reshape(S5_GROUP, S5_GROUPS, S5_STATE).transpose(1, 0, 2)
    bre = _group_block_diag(to_groups(bb_re)).astype(BF16)
    bim = _group_block_diag(to_groups(bb_im)).astype(BF16)
    cre = _group_block_diag(c_re.transpose(0, 2, 1)).astype(BF16)
    cim = _group_block_diag(c_im.transpose(0, 2, 1)).astype(BF16)
    tt = min(S5_TILE, seq)
    u3 = u_tm.reshape(seq, batch, S5_WIDTH)
    blk = pl.BlockSpec((tt, batch, S5_WIDTH), lambda i: (i, 0, 0))
    rows = tt * batch
    out = pl.pallas_call(
        functools.partial(_s5_kernel, tt=tt, nb=batch),
        grid=(seq // tt,),
        in_specs=[blk, _resident((1, S5_NSTATE)), _resident((1, S5_NSTATE)),
                  _resident(bre.shape), _resident(bim.shape), _resident(cre.shape),
                  _resident(cim.shape), _resident((1, S5_WIDTH)), _resident((S5_WIDTH, S5_WIDTH))],
        out_specs=blk,
        out_shape=jax.ShapeDtypeStruct((seq, batch, S5_WIDTH), F32),
        scratch_shapes=[pltpu.VMEM((batch, S5_NSTATE), F32), pltpu.VMEM((batch, S5_NSTATE), F32),
                        pltpu.VMEM((rows, S5_STATE_CHUNK), F32), pltpu.VMEM((rows, S5_STATE_CHUNK), F32),
                        pltpu.VMEM((rows, S5_WIDTH), F32)],
        compiler_params=_params("arbitrary"),
        name="s5",
    )(u3, abar_re, abar_im, bre, bim, cre, cim, d.reshape(1, S5_WIDTH), w_glu.astype(BF16))
    return out.reshape(seq, batch * S5_WIDTH)


def _softplus(x):
    return jnp.maximum(x, 0.0) + jnp.log(1.0 + jnp.exp(-jnp.abs(x)))


def _mm_sel(sel, x):
    x1 = x.astype(BF16)
    r1 = x - x1.astype(F32)
    x2 = r1.astype(BF16)
    x3 = (r1 - x2.astype(F32)).astype(BF16)
    s = sel.astype(BF16)
    return jnp.dot(jnp.concatenate([s, s, s], axis=1), jnp.concatenate([x1, x2, x3], axis=0),
                   preferred_element_type=F32)


def _gdn_kernel(x_ref, mnw_ref, w_ref, cw_ref, alog_ref, dtb_ref, nw_ref, o_ref, s_ref, tail_ref, *, tl):
    nc = tl // CHUNK

    @pl.when(pl.program_id(1) == 0)
    def _():
        s_ref[...] = jnp.zeros(s_ref.shape, F32)
        tail_ref[...] = jnp.zeros(tail_ref.shape, F32)

    hn = _rmsnorm(x_ref[...], mnw_ref[...]).astype(BF16)
    row_id = _iota2((tl, 1), 0)

    def shift_rows(z, carry_row):
        return jnp.where(row_id == 0, carry_row, pltpu.roll(z, 1, 0))

    def conv_silu(idx, c0):
        x = _proj(hn, w_ref, idx * GDN_QK + c0, HEAD_PAIR)
        w = cw_ref[idx, :, c0:c0 + HEAD_PAIR]
        t = tail_ref[idx, :, c0:c0 + HEAD_PAIR]
        tail_ref[idx, :, c0:c0 + HEAD_PAIR] = x[tl - SUBLANES:tl, :]
        xm1, xm2, xm3 = (t[SUBLANES - i:SUBLANES - i + 1] for i in (1, 2, 3))
        w0, w1, w2, w3 = (w[j:j + 1] for j in range(CONV_W))
        z = w0 * x
        z = w1 * x + shift_rows(z, w0 * xm1)
        z = w2 * x + shift_rows(z, w1 * xm1 + w0 * xm2)
        z = w3 * x + shift_rows(z, w2 * xm1 + w1 * xm2 + w0 * xm3)
        return _silu(z)

    pairs = []
    for pr in range(GDN_QK // HEAD_PAIR):
        c0 = pr * HEAD_PAIR
        pairs.append((conv_silu(0, c0), conv_silu(1, c0), conv_silu(2, c0),
                      _silu(_proj(hn, w_ref, 3 * GDN_QK + c0, HEAD_PAIR))))

    rt = _iota2((tl, tl), 0)
    ct = _iota2((tl, tl), 1)
    same = (rt // CHUNK) == (ct // CHUNK)
    lower = same & (rt >= ct)
    strict = same & (rt > ct)
    rs = _iota2((CHUNK, tl), 0)
    cs = _iota2((CHUNK, tl), 1)
    eye_side = (rs == cs % CHUNK).astype(F32)
    side_block = cs // CHUNK

    def to_diag(side):
        return jnp.where(same, jnp.concatenate([side] * nc, axis=0), 0.0)

    def from_diag(full):
        out = jnp.zeros((CHUNK, tl), F32)
        for c in range(nc):
            out = out + jnp.where(side_block == c, full[c * CHUNK:(c + 1) * CHUNK, :], 0.0)
        return out

    bd = _proj(hn, w_ref, 4 * GDN_QK, LANES)
    beta_all = _sigmoid(bd)
    la_all = -jnp.exp(alog_ref[...]) * _softplus(bd + dtb_ref[...])
    g_all = _mm_sel(lower, la_all)
    g_t = g_all.T
    nw = nw_ref[...]

    heads = range(GDN_HEADS)
    kn, qg, vkb, attn, gcols, n_side, t_side, gates = [], [], [], [], [], [], [], []
    for h in heads:
        sl = slice(h * GDN_DK % HEAD_PAIR, h * GDN_DK % HEAD_PAIR + GDN_DK)
        qh, kh, vh, gh = (a[:, sl] for a in pairs[h * GDN_DK // HEAD_PAIR])
        gates.append(gh)
        qh = qh * lax.rsqrt(jnp.sum(qh * qh, axis=-1, keepdims=True) + NORM_EPS) * (GDN_DK ** -0.5)
        kh = kh * lax.rsqrt(jnp.sum(kh * kh, axis=-1, keepdims=True) + NORM_EPS)
        beta = beta_all[:, h:h + 1]
        gcol = g_all[:, GDN_HEADS + h:GDN_HEADS + h + 1]
        grow = g_t[GDN_HEADS + h:GDN_HEADS + h + 1, :]
        lmask = jnp.where(lower, jnp.exp(jnp.minimum(gcol - grow, 0.0)), 0.0)
        kb = kh * beta
        kq = _mm_nt(jnp.concatenate([kb, qh], axis=0), kh)
        m = jnp.where(strict, kq[:tl] * lmask, 0.0)
        attn.append(kq[tl:] * lmask)
        eg = jnp.exp(gcol)
        n = -from_diag(m)
        n_side.append(n)
        t_side.append(eye_side + n)
        vkb.append(jnp.concatenate([vh * beta, kb * eg], axis=1))
        kn.append(kh)
        qg.append(qh * eg)
        gcols.append(gcol)

    for h in heads:
        n_side[h] = _mm(n_side[h], to_diag(n_side[h]))
    for _ in range(4):
        for h in heads:
            r = _mm(jnp.concatenate([t_side[h], n_side[h]], axis=0), to_diag(n_side[h]))
            t_side[h] = t_side[h] + r[:CHUNK]
            n_side[h] = r[CHUNK:]
    uw = []
    for h in heads:
        t_inv = t_side[h] + _mm(t_side[h], to_diag(n_side[h]))
        uw.append(_mm(to_diag(t_inv), vkb[h]))

    s = [s_ref[h] for h in heads]
    v_new = [[] for _ in heads]
    o_state = [[] for _ in heads]
    for c in range(nc):
        rows = slice(c * CHUNK, (c + 1) * CHUNK)
        ws = [_mm(jnp.concatenate([uw[h][rows, GDN_DV:], qg[h][rows]], axis=0), s[h]) for h in heads]
        for h in heads:
            vn = uw[h][rows, :GDN_DV] - ws[h][:CHUNK]
            v_new[h].append(vn)
            o_state[h].append(ws[h][CHUNK:])
            gc = gcols[h][rows]
            glast = gc[CHUNK - 1:CHUNK, :]
            s[h] = s[h] * jnp.exp(glast) + _mm_tn(kn[h][rows] * jnp.exp(glast - gc), vn)
    for h in heads:
        s_ref[h] = s[h]
        sl = slice(h * GDN_DK, (h + 1) * GDN_DK)
        o = jnp.concatenate(o_state[h], axis=0) + _mm(attn[h], jnp.concatenate(v_new[h], axis=0))
        o = o * lax.rsqrt(jnp.mean(o * o, axis=-1, keepdims=True) + NORM_EPS) * nw
        o_ref[:, sl] = o * gates[h]


def gdn(x, mix_norm_w, w_in, conv_w, a_log, dt_bias, norm_w, batch, seq):
    t = x.shape[0]
    tl = min(MIX_TILE, seq)
    steps = seq // tl
    tile = lambda n: pl.BlockSpec((tl, n), lambda b, j: (b * steps + j, 0))
    n_main = 2 * GDN_QK + 2 * GDN_V
    w_in = jnp.pad(w_in, ((0, 0), (0, LANES - (w_in.shape[1] - n_main))))
    cw = conv_w.reshape(CONV_W, 3, GDN_QK).transpose(1, 0, 2)
    lane_row = lambda p: jnp.zeros((1, LANES), F32).at[0, GDN_HEADS:2 * GDN_HEADS].set(p)
    return pl.pallas_call(
        functools.partial(_gdn_kernel, tl=tl),
        grid=(batch, steps),
        in_specs=[tile(D_MODEL), _resident((1, D_MODEL)), _resident(w_in.shape),
                  _resident((3, CONV_W, GDN_QK)), _resident((1, LANES)), _resident((1, LANES)),
                  _resident((1, GDN_DV))],
        out_specs=tile(GDN_V),
        out_shape=jax.ShapeDtypeStruct((t, GDN_V), F32),
        scratch_shapes=[pltpu.VMEM((GDN_HEADS, GDN_DK, GDN_DV), F32),
                        pltpu.VMEM((3, SUBLANES, GDN_QK), F32)],
        compiler_params=_params("parallel", "arbitrary"),
        name="gdn",
    )(x, mix_norm_w.reshape(1, D_MODEL), w_in.astype(BF16), cw, lane_row(a_log), lane_row(dt_bias),
      norm_w.reshape(1, GDN_DV))


def kernel(x, ffn1_norm, ffn1_w_gate, ffn1_w_up, ffn1_w_down, mix_norm, ffn2_norm, ffn2_w_gate, ffn2_w_up, ffn2_w_down, ev_w_in, hg_lb_logits, hg_norm_w, s5_a_re, s5_a_im, s5_b_re, s5_b_im, s5_c_re, s5_c_im, s5_d, s5_log_dt, s5_w_glu, ev_w_out, od_w_in, gdn_conv_w, gdn_a_log, gdn_dt_bias, gdn_norm_w, od_w_out, final_norm):
    batch, seq, _ = x.shape
    xf = x.reshape(batch * seq, D_MODEL)
    ffn1_w = (ffn1_w_gate, ffn1_w_up, ffn1_w_down)
    ffn2_w = (ffn2_w_gate, ffn2_w_up, ffn2_w_down)
    for layer in range(DEPTH):
        xf = ffn(xf, ffn1_norm[layer], *ffn1_w, layer, seq)
        j = layer // 2
        if layer % 2 == 0:
            y_a, u_tm = hgrn2(xf, mix_norm[layer], ev_w_in[j], hg_lb_logits, hg_norm_w[j], j, batch, seq)
            y_b = s5(u_tm, s5_a_re[j], s5_a_im[j], s5_b_re[j], s5_b_im[j], s5_c_re[j], s5_c_im[j],
                     s5_d[j], s5_log_dt[j], s5_w_glu[j], batch, seq)
            w_out = ev_w_out[j]
            parts = [(y_a, w_out[:HG_WIDTH], False), (y_b, w_out[HG_WIDTH:], True)]
        else:
            y = gdn(xf, mix_norm[layer], od_w_in[j], gdn_conv_w[j], gdn_a_log[j], gdn_dt_bias[j],
                    gdn_norm_w[j], batch, seq)
            parts = [(y, od_w_out[j], False)]
        last = layer == DEPTH - 1
        xf = ffn(xf, ffn2_norm[layer], *ffn2_w, layer, seq, parts=parts,
                 final_w=final_norm if last else None)
    return xf.reshape(batch, seq, D_MODEL)
```

```python
import functools

import jax
import jax.numpy as jnp
from jax import lax
from jax.experimental import pallas as pl
from jax.experimental.pallas import tpu as pltpu

F32 = jnp.float32
BF16 = jnp.bfloat16

D_MODEL = 1024
DEPTH = 4
D_FF = 2816
NORM_EPS = 1e-6
F_MIN = 1e-6
CHUNK = 64
HG_HEADS = 4
HG_DK = 128
HG_DV = 128
HG_QK = HG_HEADS * HG_DK
HG_WIDTH = HG_HEADS * HG_DV
S5_WIDTH = D_MODEL - HG_WIDTH
S5_GROUP = 16
S5_GROUPS = S5_WIDTH // S5_GROUP
S5_STATE = 64
S5_NSTATE = S5_GROUPS * S5_STATE
GDN_HEADS = 8
GDN_DK = 128
GDN_DV = 128
GDN_QK = GDN_HEADS * GDN_DK
GDN_V = GDN_HEADS * GDN_DV
CONV_W = 4

LANES = 128
SUBLANES = 8
VMEM_LIMIT_BYTES = 56 * 1024 * 1024

TOKEN_TILE = 512
FF_BLOCK = 256
MIX_TILE = 256
S5_TILE = 64
S5_LANE_CHUNK = 128
S5_STATE_CHUNK = S5_LANE_CHUNK // S5_GROUP * S5_STATE


def _mm(a, b):
    return jnp.dot(a.astype(BF16), b.astype(BF16), preferred_element_type=F32)


def _mm_nt(a, b):
    return lax.dot_general(a.astype(BF16), b.astype(BF16), (((1,), (1,)), ((), ())),
                           preferred_element_type=F32)


def _mm_tn(a, b):
    return lax.dot_general(a.astype(BF16), b.astype(BF16), (((0,), (0,)), ((), ())),
                           preferred_element_type=F32)


def _sigmoid(x):
    return 1.0 / (1.0 + jnp.exp(-x))


def _silu(x):
    return x * _sigmoid(x)


def _rmsnorm(x, w):
    return x * lax.rsqrt(jnp.mean(x * x, axis=-1, keepdims=True) + NORM_EPS) * w


def _resident(shape):
    nd = len(shape)
    return pl.BlockSpec(shape, lambda *_: (0,) * nd, pipeline_mode=pl.Buffered(1))


def _params(*sem):
    return pltpu.CompilerParams(dimension_semantics=sem, vmem_limit_bytes=VMEM_LIMIT_BYTES)


def _ffn_kernel(*refs, n_parts, final):
    x_ref, refs = refs[0], refs[1:]
    part_refs, refs = refs[:2 * n_parts], refs[2 * n_parts:]
    if final:
        nw_ref, wg_ref, wu_ref, wd_ref, fw_ref, o_ref = refs
    else:
        nw_ref, wg_ref, wu_ref, wd_ref, o_ref = refs
    x = x_ref[...]
    for y_ref, w_ref in zip(part_refs[0::2], part_refs[1::2]):
        x = x + jnp.dot(y_ref[...].astype(BF16), w_ref[...], preferred_element_type=F32)
    h = _rmsnorm(x, nw_ref[...])
    acc = jnp.zeros(x.shape, F32)
    for j in range(D_FF // FF_BLOCK):
        sl = slice(j * FF_BLOCK, (j + 1) * FF_BLOCK)
        g = jnp.dot(h, wg_ref[:, sl], preferred_element_type=F32)
        u = jnp.dot(h, wu_ref[:, sl], preferred_element_type=F32)
        a = _silu(g) * u
        acc = acc + jnp.dot(a, wd_ref[sl, :], preferred_element_type=F32)
    y = x + 0.5 * acc
    if final:
        y = _rmsnorm(y, fw_ref[...])
    o_ref[...] = y


def _resident_layer(shape, layer):
    nd = len(shape) - 1
    return pl.BlockSpec((None,) + tuple(shape[1:]), lambda *_: (layer,) + (0,) * nd,
                        pipeline_mode=pl.Buffered(1))


def ffn(x, norm_w, w_gate, w_up, w_down, layer, seq, parts=(), final_w=None):
    t = x.shape[0]
    tm = min(TOKEN_TILE, seq)
    blocks_per_seq = seq // tm
    final = final_w is not None
    row = pl.BlockSpec((tm, D_MODEL), lambda i: (i, 0))
    in_specs, args = [row], [x]
    for y, w, time_major in parts:
        n = w.shape[0]
        if time_major:
            in_specs.append(pl.BlockSpec(
                (tm, n), lambda i: (i % blocks_per_seq, i // blocks_per_seq)))
        else:
            in_specs.append(pl.BlockSpec((tm, n), lambda i: (i, 0)))
        in_specs.append(_resident((n, D_MODEL)))
        args += [y, w.astype(BF16)]
    in_specs += [_resident((1, D_MODEL))] + [_resident_layer(w.shape, layer) for w in (w_gate, w_up, w_down)]
    args += [norm_w.reshape(1, D_MODEL), w_gate, w_up, w_down]
    if final:
        in_specs.append(_resident((1, D_MODEL)))
        args.append(final_w.reshape(1, D_MODEL))
    return pl.pallas_call(
        functools.partial(_ffn_kernel, n_parts=len(parts), final=final),
        grid=(t // tm,),
        in_specs=in_specs,
        out_specs=row,
        out_shape=jax.ShapeDtypeStruct((t, D_MODEL), F32),
        compiler_params=_params("parallel"),
        name=("ffn_mix" if parts else "ffn") + ("_final" if final else ""),
    )(*args)


def _proj(hn, w_ref, start, width):
    return jnp.dot(hn, w_ref[:, start:start + width], preferred_element_type=F32)


HEAD_PAIR = 2 * LANES


def _iota2(shape, axis):
    return lax.broadcasted_iota(jnp.int32, shape, axis)


def _hgrn2_kernel(x_ref, mnw_ref, w_ref, lbl_ref, nw_ref, o_ref, u_ref, st_ref, *, layer, tl):
    nc = tl // CHUNK
    hn = _rmsnorm(x_ref[...], mnw_ref[...]).astype(BF16)
    u_ref[...] = _proj(hn, w_ref, 2 * HG_QK + 2 * HG_WIDTH, S5_WIDTH)

    @pl.when(pl.program_id(1) == 0)
    def _():
        st_ref[...] = jnp.zeros(st_ref.shape, F32)

    logits = lbl_ref[...]
    e = jnp.exp(logits - jnp.max(logits, axis=0, keepdims=True))
    p = e / jnp.sum(e, axis=0, keepdims=True)
    lb = jnp.sum(p[0:layer + 1], axis=0, keepdims=True) - p[0:1]
    nw = nw_ref[...]

    rt = _iota2((tl, tl), 0)
    ct = _iota2((tl, tl), 1)
    chunk_lower = ((rt // CHUNK) == (ct // CHUNK)) & (rt >= ct)
    levels = (32, 16, 8, 4)
    level_mask = {m: ((rt // (2 * m)) == (ct // (2 * m))) & ((rt // m) % 2 == 1) & ((ct // m) % 2 == 0)
                  for m in levels}
    base = 4
    base_mask = ((rt // base) == (ct // base)) & (rt >= ct)
    pos = _iota2((tl, 1), 0) % base

    pairs = []
    for pr in range(HG_QK // HEAD_PAIR):
        c0 = pr * HEAD_PAIR
        lbp = lb[:, c0:c0 + HEAD_PAIR]
        q = _silu(_proj(hn, w_ref, c0, HEAD_PAIR))
        f = lbp + (1.0 - lbp) * _sigmoid(_proj(hn, w_ref, HG_QK + c0, HEAD_PAIR))
        lf = jnp.log(jnp.maximum(f, F_MIN))
        v = _proj(hn, w_ref, 2 * HG_QK + c0, HEAD_PAIR)
        gate = _silu(_proj(hn, w_ref, 2 * HG_QK + HG_WIDTH + c0, HEAD_PAIR))
        b = _mm_sel(chunk_lower, lf)
        c4 = lf
        for j in range(1, base):
            c4 = c4 + jnp.where(pos >= j, pltpu.roll(lf, j, 0), 0.0)
        pairs.append((b, q, 1.0 - f, v, c4, gate))

    outs = []
    for h in range(HG_HEADS):
        sl = slice(h * HG_DK % HEAD_PAIR, h * HG_DK % HEAD_PAIR + HG_DK)
        bh, qh, kh, vh, ch, gh = (a[:, sl] for a in pairs[h * HG_DK // HEAD_PAIR])
        attn = jnp.where(base_mask, _mm_nt(qh * jnp.exp(ch), kh * jnp.exp(-ch)), 0.0)
        for m in levels:
            if m == base:
                b3 = bh.reshape(tl // (2 * m), 2 * m, HG_DK)
                ref_q = ref_k = b3[:, m - 1:m, :]
            else:
                b3 = bh.reshape(tl // m, m, HG_DK)
                ref_k = b3[:, m - 1:m, :]
                ref_q = jnp.concatenate([jnp.zeros((1, 1, HG_DK), F32), ref_k[:-1]], axis=0)
            qt = qh * jnp.exp(jnp.minimum(b3 - ref_q, 0.0)).reshape(tl, HG_DK)
            kt = kh * jnp.exp(jnp.minimum(ref_k - b3, 0.0)).reshape(tl, HG_DK)
            attn = jnp.where(level_mask[m], _mm_nt(qt, kt), attn)
        o = _mm(attn, vh)
        bc = bh.reshape(nc, CHUNK, HG_DK)
        blast = bc[:, CHUNK - 1:CHUNK, :]
        ks = kh * jnp.exp(blast - bc).reshape(tl, HG_DK)
        qe = qh * jnp.exp(bh)
        st = st_ref[h]
        o_state = []
        for c in range(nc):
            rows = slice(c * CHUNK, (c + 1) * CHUNK)
            o_state.append(_mm_nt(qe[rows], st))
            st = st * jnp.exp(blast[c]) + _mm_tn(vh[rows], ks[rows])
        st_ref[h] = st
        o = o + jnp.concatenate(o_state, axis=0)
        o = o * lax.rsqrt(jnp.mean(o * o, axis=-1, keepdims=True) + NORM_EPS) * nw
        outs.append(o * gh)
    o_ref[...] = jnp.concatenate(outs, axis=-1)


def hgrn2(x, mix_norm_w, w_in, lb_logits, norm_w, layer, batch, seq):
    t = x.shape[0]
    tl = min(MIX_TILE, seq)
    steps = seq // tl
    n_layers = lb_logits.shape[0]
    return pl.pallas_call(
        functools.partial(_hgrn2_kernel, layer=layer, tl=tl),
        grid=(batch, steps),
        in_specs=[pl.BlockSpec((tl, D_MODEL), lambda b, j: (b * steps + j, 0)), _resident((1, D_MODEL)),
                  _resident(w_in.shape), _resident((n_layers, HG_QK)), _resident((1, HG_DV))],
        out_specs=[pl.BlockSpec((tl, HG_WIDTH), lambda b, j: (b * steps + j, 0)),
                   pl.BlockSpec((tl, S5_WIDTH), lambda b, j: (j, b))],
        out_shape=[jax.ShapeDtypeStruct((t, HG_WIDTH), F32),
                   jax.ShapeDtypeStruct((seq, batch * S5_WIDTH), F32)],
        scratch_shapes=[pltpu.VMEM((HG_HEADS, HG_DV, HG_DK), F32)],
        compiler_params=_params("parallel", "arbitrary"),
        name="hgrn2",
    )(x, mix_norm_w.reshape(1, D_MODEL), w_in.astype(BF16), lb_logits, norm_w.reshape(1, HG_DV))


def _s5_prep_kernel(are_ref, aim_ref, ldt_ref, bre_ref, bim_ref, abr_ref, abi_ref, bbr_ref, bbi_ref):
    a_re, a_im = are_ref[...], aim_ref[...]
    dt = jnp.exp(ldt_ref[...])
    mag = jnp.exp(dt * a_re)
    ang = dt * a_im
    abar_re = mag * jnp.cos(ang)
    abar_im = mag * jnp.sin(ang)
    den = a_re * a_re + a_im * a_im
    zr = abar_re - 1.0
    zi = abar_im
    coef_re = (zr * a_re + zi * a_im) / den
    coef_im = (zi * a_re - zr * a_im) / den
    b_re, b_im = bre_ref[...], bim_ref[...]
    abr_ref[...] = abar_re
    abi_ref[...] = abar_im
    bbr_ref[...] = coef_re * b_re - coef_im * b_im
    bbi_ref[...] = coef_re * b_im + coef_im * b_re


def s5_prep(a_re, a_im, log_dt, b_re, b_im):
    flat = lambda a: a.reshape(1, S5_NSTATE)
    ldt = jnp.broadcast_to(log_dt[:, None], (S5_GROUPS, S5_STATE))
    by_p = lambda b: b.transpose(2, 0, 1).reshape(S5_GROUP, S5_NSTATE)
    row = jax.ShapeDtypeStruct((1, S5_NSTATE), F32)
    mat = jax.ShapeDtypeStruct((S5_GROUP, S5_NSTATE), F32)
    return pl.pallas_call(_s5_prep_kernel, out_shape=[row, row, mat, mat], name="s5_prep")(
        flat(a_re), flat(a_im), flat(ldt), by_p(b_re), by_p(b_im))


def _s5_kernel(u_ref, abr_ref, abi_ref, bre_ref, bim_ref, cre_ref, cim_ref, d_ref, wglu_ref, o_ref,
               hre_ref, him_ref, sre_ref, sim_ref, y_ref, *, tt, nb):
    @pl.when(pl.program_id(0) == 0)
    def _():
        hre_ref[...] = jnp.zeros(hre_ref.shape, F32)
        him_ref[...] = jnp.zeros(him_ref.shape, F32)

    u2 = u_ref[...]
    u = jnp.stack([u2[:, b * S5_WIDTH:(b + 1) * S5_WIDTH] for b in range(nb)], axis=1)
    u = u.reshape(tt * nb, S5_WIDTH)
    for j in range(S5_WIDTH // S5_LANE_CHUNK):
        uj = u[:, j * S5_LANE_CHUNK:(j + 1) * S5_LANE_CHUNK].astype(BF16)
        ssl = slice(j * S5_STATE_CHUNK, (j + 1) * S5_STATE_CHUNK)
        sre_ref[...] = jnp.dot(uj, bre_ref[j], preferred_element_type=F32)
        sim_ref[...] = jnp.dot(uj, bim_ref[j], preferred_element_type=F32)
        ar = abr_ref[:, ssl]
        ai = abi_ref[:, ssl]

        def step(t, carry):
            hr, hi = carry
            rows = pl.ds(pl.multiple_of(t * nb, nb), nb)
            nr = ar * hr - ai * hi + sre_ref[rows, :]
            ni = ar * hi + ai * hr + sim_ref[rows, :]
            sre_ref[rows, :] = nr
            sim_ref[rows, :] = ni
            return nr, ni

        hr, hi = lax.fori_loop(0, tt, step, (hre_ref[:, ssl], him_ref[:, ssl]), unroll=4)
        hre_ref[:, ssl] = hr
        him_ref[:, ssl] = hi
        y_ref[:, j * S5_LANE_CHUNK:(j + 1) * S5_LANE_CHUNK] = (
            jnp.dot(sre_ref[...].astype(BF16), cre_ref[j], preferred_element_type=F32)
            - jnp.dot(sim_ref[...].astype(BF16), cim_ref[j], preferred_element_type=F32))
    y = jax.nn.gelu(y_ref[...] + d_ref[...] * u)
    out = y * _sigmoid(jnp.dot(y.astype(BF16), wglu_ref[...], preferred_element_type=F32))
    out3 = out.reshape(tt, nb, S5_WIDTH)
    for b in range(nb):
        o_ref[:, b * S5_WIDTH:(b + 1) * S5_WIDTH] = out3[:, b, :]


def _group_block_diag(m):
    per = S5_LANE_CHUNK // S5_GROUP
    g, r, c = m.shape
    m = m.reshape(g // per, per, r, c)
    eye = jnp.eye(per, dtype=m.dtype)
    return jnp.einsum("jgrc,gh->jgrhc", m, eye).reshape(g // per, per * r, per * c)


def s5(u_tm, a_re, a_im, b_re, b_im, c_re, c_im, d, log_dt, w_glu, batch, seq):
    abar_re, abar_im, bb_re, bb_im = s5_prep(a_re, a_im, log_dt, b_re, b_im)
    to_groups = lambda bb: bb.reshape(S5_GROUP, S5_GROUPS, S5_STATE).transpose(1, 0, 2)
    bre = _group_block_diag(to_groups(bb_re)).astype(BF16)
    bim = _group_block_diag(to_groups(bb_im)).astype(BF16)
    cre = _group_block_diag(c_re.transpose(0, 2, 1)).astype(BF16)
    cim = _group_block_diag(c_im.transpose(0, 2, 1)).astype(BF16)
    tt = min(S5_TILE, seq)
    blk = pl.BlockSpec((tt, batch * S5_WIDTH), lambda i: (i, 0))
    rows = tt * batch
    return pl.pallas_call(
        functools.partial(_s5_kernel, tt=tt, nb=batch),
        grid=(seq // tt,),
        in_specs=[blk, _resident((1, S5_NSTATE)), _resident((1, S5_NSTATE)),
                  _resident(bre.shape), _resident(bim.shape), _resident(cre.shape),
                  _resident(cim.shape), _resident((1, S5_WIDTH)), _resident((S5_WIDTH, S5_WIDTH))],
        out_specs=blk,
        out_shape=jax.ShapeDtypeStruct((seq, batch * S5_WIDTH), F32),
        scratch_shapes=[pltpu.VMEM((batch, S5_NSTATE), F32), pltpu.VMEM((batch, S5_NSTATE), F32),
                        pltpu.VMEM((rows, S5_STATE_CHUNK), F32), pltpu.VMEM((rows, S5_STATE_CHUNK), F32),
                        pltpu.VMEM((rows, S5_WIDTH), F32)],
        compiler_params=_params("arbitrary"),
        name="s5",
    )(u_tm, abar_re, abar_im, bre, bim, cre, cim, d.reshape(1, S5_WIDTH), w_glu.astype(BF16))


def _softplus(x):
    return jnp.maximum(x, 0.0) + jnp.log(1.0 + jnp.exp(-jnp.abs(x)))


def _mm_sel(sel, x):
    x1 = x.astype(BF16)
    r1 = x - x1.astype(F32)
    x2 = r1.astype(BF16)
    x3 = (r1 - x2.astype(F32)).astype(BF16)
    s = sel.astype(BF16)
    return jnp.dot(jnp.concatenate([s, s, s], axis=1), jnp.concatenate([x1, x2, x3], axis=0),
                   preferred_element_type=F32)


def _gdn_kernel(x_ref, mnw_ref, w_ref, cw_ref, alog_ref, dtb_ref, nw_ref, o_ref, s_ref, tail_ref, *, tl):
    nc = tl // CHUNK

    @pl.when(pl.program_id(1) == 0)
    def _():
        s_ref[...] = jnp.zeros(s_ref.shape, F32)
        tail_ref[...] = jnp.zeros(tail_ref.shape, F32)

    hn = _rmsnorm(x_ref[...], mnw_ref[...]).astype(BF16)
    row_id = _iota2((tl, 1), 0)

    def shift_rows(z, carry_row):
        return jnp.where(row_id == 0, carry_row, pltpu.roll(z, 1, 0))

    def conv_silu(idx, c0):
        x = _proj(hn, w_ref, idx * GDN_QK + c0, HEAD_PAIR)
        w = cw_ref[idx, :, c0:c0 + HEAD_PAIR]
        t = tail_ref[idx, :, c0:c0 + HEAD_PAIR]
        tail_ref[idx, :, c0:c0 + HEAD_PAIR] = x[tl - SUBLANES:tl, :]
        xm1, xm2, xm3 = (t[SUBLANES - i:SUBLANES - i + 1] for i in (1, 2, 3))
        w0, w1, w2, w3 = (w[j:j + 1] for j in range(CONV_W))
        z = w0 * x
        z = w1 * x + shift_rows(z, w0 * xm1)
        z = w2 * x + shift_rows(z, w1 * xm1 + w0 * xm2)
        z = w3 * x + shift_rows(z, w2 * xm1 + w1 * xm2 + w0 * xm3)
        return _silu(z)

    pairs = []
    for pr in range(GDN_QK // HEAD_PAIR):
        c0 = pr * HEAD_PAIR
        pairs.append((conv_silu(0, c0), conv_silu(1, c0), conv_silu(2, c0),
                      _silu(_proj(hn, w_ref, 3 * GDN_QK + c0, HEAD_PAIR))))

    rt = _iota2((tl, tl), 0)
    ct = _iota2((tl, tl), 1)
    same = (rt // CHUNK) == (ct // CHUNK)
    lower = same & (rt >= ct)
    strict = same & (rt > ct)
    rs = _iota2((CHUNK, tl), 0)
    cs = _iota2((CHUNK, tl), 1)
    eye_side = (rs == cs % CHUNK).astype(F32)
    side_block = cs // CHUNK

    def to_diag(side):
        return jnp.where(same, jnp.concatenate([side] * nc, axis=0), 0.0)

    def from_diag(full):
        out = jnp.zeros((CHUNK, tl), F32)
        for c in range(nc):
            out = out + jnp.where(side_block == c, full[c * CHUNK:(c + 1) * CHUNK, :], 0.0)
        return out

    bd = _proj(hn, w_ref, 4 * GDN_QK, LANES)
    beta_all = _sigmoid(bd)
    la_all = -jnp.exp(alog_ref[...]) * _softplus(bd + dtb_ref[...])
    g_all = _mm_sel(lower, la_all)
    g_t = g_all.T
    nw = nw_ref[...]

    heads = range(GDN_HEADS)
    kn, qg, vkb, attn, gcols, n_side, t_side, gates = [], [], [], [], [], [], [], []
    for h in heads:
        sl = slice(h * GDN_DK % HEAD_PAIR, h * GDN_DK % HEAD_PAIR + GDN_DK)
        qh, kh, vh, gh = (a[:, sl] for a in pairs[h * GDN_DK // HEAD_PAIR])
        gates.append(gh)
        qh = qh * lax.rsqrt(jnp.sum(qh * qh, axis=-1, keepdims=True) + NORM_EPS) * (GDN_DK ** -0.5)
        kh = kh * lax.rsqrt(jnp.sum(kh * kh, axis=-1, keepdims=True) + NORM_EPS)
        beta = beta_all[:, h:h + 1]
        gcol = g_all[:, GDN_HEADS + h:GDN_HEADS + h + 1]
        grow = g_t[GDN_HEADS + h:GDN_HEADS + h + 1, :]
        lmask = jnp.where(lower, jnp.exp(jnp.minimum(gcol - grow, 0.0)), 0.0)
        kb = kh * beta
        kq = _mm_nt(jnp.concatenate([kb, qh], axis=0), kh)
        m = jnp.where(strict, kq[:tl] * lmask, 0.0)
        attn.append(kq[tl:] * lmask)
        eg = jnp.exp(gcol)
        n = -from_diag(m)
        n_side.append(n)
        t_side.append(eye_side + n)
        vkb.append(jnp.concatenate([vh * beta, kb * eg], axis=1))
        kn.append(kh)
        qg.append(qh * eg)
        gcols.append(gcol)

    for h in heads:
        n_side[h] = _mm(n_side[h], to_diag(n_side[h]))
    for _ in range(4):
        for h in heads:
            r = _mm(jnp.concatenate([t_side[h], n_side[h]], axis=0), to_diag(n_side[h]))
            t_side[h] = t_side[h] + r[:CHUNK]
            n_side[h] = r[CHUNK:]
    uw = []
    for h in heads:
        t_inv = t_side[h] + _mm(t_side[h], to_diag(n_side[h]))
        uw.append(_mm(to_diag(t_inv), vkb[h]))

    s = [s_ref[h] for h in heads]
    v_new = [[] for _ in heads]
    o_state = [[] for _ in heads]
    for c in range(nc):
        rows = slice(c * CHUNK, (c + 1) * CHUNK)
        ws = [_mm(jnp.concatenate([uw[h][rows, GDN_DV:], qg[h][rows]], axis=0), s[h]) for h in heads]
        for h in heads:
            vn = uw[h][rows, :GDN_DV] - ws[h][:CHUNK]
            v_new[h].append(vn)
            o_state[h].append(ws[h][CHUNK:])
            gc = gcols[h][rows]
            glast = gc[CHUNK - 1:CHUNK, :]
            s[h] = s[h] * jnp.exp(glast) + _mm_tn(kn[h][rows] * jnp.exp(glast - gc), vn)
    for h in heads:
        s_ref[h] = s[h]
        sl = slice(h * GDN_DK, (h + 1) * GDN_DK)
        o = jnp.concatenate(o_state[h], axis=0) + _mm(attn[h], jnp.concatenate(v_new[h], axis=0))
        o = o * lax.rsqrt(jnp.mean(o * o, axis=-1, keepdims=True) + NORM_EPS) * nw
        o_ref[:, sl] = o * gates[h]


def gdn(x, mix_norm_w, w_in, conv_w, a_log, dt_bias, norm_w, batch, seq):
    t = x.shape[0]
    tl = min(MIX_TILE, seq)
    steps = seq // tl
    tile = lambda n: pl.BlockSpec((tl, n), lambda b, j: (b * steps + j, 0))
    n_main = 2 * GDN_QK + 2 * GDN_V
    w_in = jnp.pad(w_in, ((0, 0), (0, LANES - (w_in.shape[1] - n_main))))
    cw = conv_w.reshape(CONV_W, 3, GDN_QK).transpose(1, 0, 2)
    lane_row = lambda p: jnp.zeros((1, LANES), F32).at[0, GDN_HEADS:2 * GDN_HEADS].set(p)
    return pl.pallas_call(
        functools.partial(_gdn_kernel, tl=tl),
        grid=(batch, steps),
        in_specs=[tile(D_MODEL), _resident((1, D_MODEL)), _resident(w_in.shape),
                  _resident((3, CONV_W, GDN_QK)), _resident((1, LANES)), _resident((1, LANES)),
                  _resident((1, GDN_DV))],
        out_specs=tile(GDN_V),
        out_shape=jax.ShapeDtypeStruct((t, GDN_V), F32),
        scratch_shapes=[pltpu.VMEM((GDN_HEADS, GDN_DK, GDN_DV), F32),
                        pltpu.VMEM((3, SUBLANES, GDN_QK), F32)],
        compiler_params=_params("parallel", "arbitrary"),
        name="gdn",
    )(x, mix_norm_w.reshape(1, D_MODEL), w_in.astype(BF16), cw, lane_row(a_log), lane_row(dt_bias),
      norm_w.reshape(1, GDN_DV))


def kernel(x, ffn1_norm, ffn1_w_gate, ffn1_w_up, ffn1_w_down, mix_norm, ffn2_norm, ffn2_w_gate, ffn2_w_up, ffn2_w_down, ev_w_in, hg_lb_logits, hg_norm_w, s5_a_re, s5_a_im, s5_b_re, s5_b_im, s5_c_re, s5_c_im, s5_d, s5_log_dt, s5_w_glu, ev_w_out, od_w_in, gdn_conv_w, gdn_a_log, gdn_dt_bias, gdn_norm_w, od_w_out, final_norm):
    batch, seq, _ = x.shape
    xf = x.reshape(batch * seq, D_MODEL)
    ffn1_w = (ffn1_w_gate, ffn1_w_up, ffn1_w_down)
    ffn2_w = (ffn2_w_gate, ffn2_w_up, ffn2_w_down)
    for layer in range(DEPTH):
        xf = ffn(xf, ffn1_norm[layer], *ffn1_w, layer, seq)
        j = layer // 2
        if layer % 2 == 0:
            y_a, u_tm = hgrn2(xf, mix_norm[layer], ev_w_in[j], hg_lb_logits, hg_norm_w[j], j, batch, seq)
            y_b = s5(u_tm, s5_a_re[j], s5_a_im[j], s5_b_re[j], s5_b_im[j], s5_c_re[j], s5_c_im[j],
                     s5_d[j], s5_log_dt[j], s5_w_glu[j], batch, seq)
            w_out = ev_w_out[j]
            parts = [(y_a, w_out[:HG_WIDTH], False), (y_b, w_out[HG_WIDTH:], True)]
        else:
            y = gdn(xf, mix_norm[layer], od_w_in[j], gdn_conv_w[j], gdn_a_log[j], gdn_dt_bias[j],
                    gdn_norm_w[j], batch, seq)
            parts = [(y, od_w_out[j], False)]
        last = layer == DEPTH - 1
        xf = ffn(xf, ffn2_norm[layer], *ffn2_w, layer, seq, parts=parts,
                 final_w=final_norm if last else None)
    return xf.reshape(batch, seq, D_MODEL)
```
